```python
import math
import jax
import jax.numpy as jnp
from jax import lax
import numpy as np

D_MODEL = 2048
BATCH = 4
SEQ = 4096
DEPTH = 4

GRID_W = 64
CTX_LEN = 256
NORM_EPS = 1e-6

RW_HEAD_DIM = 64
RW_WIDTH = D_MODEL // 2
RW_HEADS = RW_WIDTH // RW_HEAD_DIM
DECAY_LORA = 64
ICL_LORA = 64
GATE_LORA = 128
CONV_W = 3
GN_EPS = 64e-5
L2_EPS = 1e-12

DA_HEAD_DIM = 64
DA_V_DIM = 2 * DA_HEAD_DIM
DA_WIDTH = D_MODEL - RW_WIDTH
DA_HEADS = DA_WIDTH // DA_V_DIM
DA_QK = DA_HEADS * 2 * DA_HEAD_DIM
ROPE_FREQS = DA_HEAD_DIM // 4
ROPE_THETA = 10000.0
Q_BLOCK = 128

MIX_WIDTH = RW_WIDTH + DA_WIDTH
IN_W = 3 * RW_WIDTH + DECAY_LORA + ICL_LORA + GATE_LORA + 2 * DA_QK + DA_WIDTH

PEER_HEADS = 8
N_KEYS = 128
N_EXPERTS = N_KEYS * N_KEYS
PEER_TOPK = 16
PEER_QDIM = 256
PEER_HALF = PEER_QDIM // 2
TOK_BLOCK = 128

kernel_name = 'hybrid_rwkv7_diffattn_peer_dit'


def rms_norm(x, gain):
    xf = x.astype(jnp.float32)
    y = xf * lax.rsqrt(jnp.mean(xf * xf, axis=-1, keepdims=True) + NORM_EPS)
    return (y * gain.astype(jnp.float32)).astype(x.dtype)


def modulate(h, shift, scale):
    return h * (1 + scale) + shift


def centred_conv(x, w):
    pad = CONV_W // 2
    t = x.shape[1]
    xp = jnp.pad(x, ((0, 0), (pad, pad), (0, 0)))
    return sum(xp[:, j:j + t] * w[j] for j in range(CONV_W))


def axial_rope_tables(n_tokens):
    rows = n_tokens // GRID_W
    row = jnp.repeat(jnp.arange(rows, dtype=jnp.int32), GRID_W)
    col = jnp.tile(jnp.arange(GRID_W, dtype=jnp.int32), rows)
    pos = jnp.stack([row, col], axis=-1).astype(jnp.float32)
    inv_freq = ROPE_THETA ** (-jnp.arange(ROPE_FREQS, dtype=jnp.float32) / ROPE_FREQS)
    ang = pos[:, :, None] * inv_freq
    return jnp.cos(ang), jnp.sin(ang)


def apply_axial_rope(x, cos, sin):
    shp = x.shape
    xr = x.astype(jnp.float32).reshape(shp[:-1] + (2, 2, ROPE_FREQS))
    c = cos[:, None, None]
    s = sin[:, None, None]
    x1, x2 = xr[..., 0, :], xr[..., 1, :]
    out = jnp.stack([x1 * c - x2 * s, x2 * c + x1 * s], axis=-2)
    return out.reshape(shp).astype(x.dtype)


def split_proj(p, conv_w):
    rkv = centred_conv(p[..., :3 * RW_WIDTH], conv_w)
    r, k, v = jnp.split(rkv, 3, axis=-1)
    tail = p[..., 3 * RW_WIDTH:]
    c0 = DECAY_LORA
    c1 = c0 + ICL_LORA
    c2 = c1 + GATE_LORA
    c3 = c2 + DA_QK
    c4 = c3 + DA_QK
    w_lo, a_lo, g_lo, dq, dk, dv = jnp.split(tail, [c0, c1, c2, c3, c4], axis=-1)
    return (r, k, v, w_lo, a_lo, g_lo), (dq, dk, dv)


def rwkv7_scan(state0, r, decay, k, v, kap, a, reverse):
    def step(s, inp):
        r_t, w_t, k_t, v_t, kap_t, a_t = inp
        sa = jnp.einsum('bhvk,bhk->bhv', s, kap_t)
        s = (s * w_t[:, :, None, :] - sa[..., None] * (kap_t * a_t)[:, :, None, :]
             + v_t[..., None] * k_t[:, :, None, :])
        return s, jnp.einsum('bhvk,bhk->bhv', s, r_t)
    xs = tuple(jnp.moveaxis(t, 1, 0) for t in (r, decay, k, v, kap, a))
    s_fin, y = lax.scan(step, state0, xs, reverse=reverse)
    return s_fin, jnp.moveaxis(y, 0, 1)


def rwkv7_group(lat, ctx, w0, w_up, a0, a_up, g_up, k_k, k_a, r_k, gn_gain, gn_bias, need_ctx):
    f32 = jnp.float32

    def heads(t):
        return t.astype(f32).reshape(t.shape[:2] + (RW_HEADS, RW_HEAD_DIM))

    def direction_inputs(s, d):
        r, k, v, w_lo, a_lo, _ = s
        log_w = -jax.nn.softplus(-(w0[d] + jnp.tanh(w_lo) @ w_up[d]).astype(f32)) - 0.5
        decay = jnp.exp(-jnp.exp(log_w))
        a = jax.nn.sigmoid((a0[d] + a_lo @ a_up[d]).astype(f32))
        kap = heads(k * k_k)
        kap = kap / jnp.maximum(jnp.linalg.norm(kap, axis=-1, keepdims=True), L2_EPS)
        k_mod = heads(k.astype(f32) * (1 + (a - 1) * k_a.astype(f32)))
        return heads(r), heads(decay), k_mod, heads(v), kap, heads(a)

    def bonus(inp):
        r, _, k_mod, v, _, _ = inp
        return jnp.sum(r * k_mod * r_k.astype(f32), axis=-1, keepdims=True) * v

    def finish(s, y, b):
        mu = jnp.mean(y, axis=-1, keepdims=True)
        var = jnp.mean(jnp.square(y - mu), axis=-1, keepdims=True)
        yn = (y - mu) * lax.rsqrt(var + GN_EPS) + b
        yn = yn.reshape(y.shape[:2] + (RW_WIDTH,))
        yn = (yn - b.reshape(yn.shape)) * gn_gain.astype(f32) + gn_bias.astype(f32) + b.reshape(yn.shape)
        g = jax.nn.sigmoid(s[5].astype(f32)) @ g_up.astype(f32)
        return (yn * g).astype(s[0].dtype)

    bsz = lat[0].shape[0]
    s0 = jnp.zeros((bsz, RW_HEADS, RW_HEAD_DIM, RW_HEAD_DIM), f32)
    y_lat = b_lat = y_ctx = b_ctx = 0.0
    for d in range(2):
        rev = d == 1
        ic = direction_inputs(ctx, d)
        s_ctx, yc = rwkv7_scan(s0, *ic, reverse=rev)
        il = direction_inputs(lat, d)
        _, yl = rwkv7_scan(s_ctx, *il, reverse=rev)
        y_lat = y_lat + yl
        b_lat = b_lat + bonus(il)
        if need_ctx:
            y_ctx = y_ctx + yc
            b_ctx = b_ctx + bonus(ic)
    out_lat = finish(lat, y_lat, b_lat)
    out_ctx = finish(ctx, y_ctx, b_ctx) if need_ctx else None
    return out_lat, out_ctx


def diff_attention_group(lat, ctx, q_gain, k_gain, lam_vec, sub_gain, lam_init, cos, sin, need_ctx):
    f32 = jnp.float32
    scale = DA_HEAD_DIM ** -0.5

    def qkv(dq, dk, dv):
        bsz, t = dq.shape[:2]
        q = rms_norm(dq.reshape(bsz, t, DA_HEADS, 2, DA_HEAD_DIM), q_gain)
        k = rms_norm(dk.reshape(bsz, t, DA_HEADS, 2, DA_HEAD_DIM), k_gain)
        return q, k, dv.reshape(bsz, t, DA_HEADS, DA_V_DIM)

    q, k, v = qkv(*lat)
    q = apply_axial_rope(q, cos, sin)
    k = apply_axial_rope(k, cos, sin)
    qc, kc, vc = qkv(*ctx)
    lv = lam_vec.astype(f32)
    lam = jnp.exp(jnp.sum(lv[0] * lv[1])) - jnp.exp(jnp.sum(lv[2] * lv[3])) + lam_init

    def attend(qb, keys, vals):
        s = jnp.einsum('bqhmd,bkhmd->bhmqk', qb.astype(f32), keys) * scale
        p = jax.nn.softmax(s, axis=-1)
        amap = p[:, :, 0] - lam * p[:, :, 1]
        return jnp.einsum('bhqk,bkhv->bqhv', amap, vals)

    def post(o):
        bsz, t = o.shape[:2]
        o = rms_norm(o, sub_gain) * (1 - lam_init)
        return o.reshape(bsz, t, DA_WIDTH).astype(v.dtype)

    kc32, vc32 = kc.astype(f32), vc.astype(f32)
    keys = jnp.concatenate([k.astype(f32), kc32], axis=1)
    vals = jnp.concatenate([v.astype(f32), vc32], axis=1)
    bsz, t = q.shape[:2]
    nb = t // Q_BLOCK
    qb = jnp.moveaxis(q.reshape(bsz, nb, Q_BLOCK, DA_HEADS, 2, DA_HEAD_DIM), 1, 0)
    o = lax.map(lambda blk: attend(blk, keys, vals), qb)
    out_lat = post(jnp.moveaxis(o, 0, 1).reshape(bsz, t, DA_HEADS, DA_V_DIM))
    out_ctx = post(attend(qc, kc32, vc32)) if need_ctx else None
    return out_lat, out_ctx


def peer_ffn(h, wq, keys, u_tab, v_tab):
    bsz, t, d = h.shape
    f32 = jnp.float32

    def block(hb):
        q = (hb @ wq).astype(f32).reshape(TOK_BLOCK, PEER_HEADS, 2, PEER_HALF)
        s = jnp.einsum('thpd,hpkd->thpk', q, keys.astype(f32))
        s1, i1 = lax.top_k(s[:, :, 0], PEER_TOPK)
        s2, i2 = lax.top_k(s[:, :, 1], PEER_TOPK)
        cand = (s1[..., :, None] + s2[..., None, :]).reshape(TOK_BLOCK, PEER_HEADS, PEER_TOPK * PEER_TOPK)
        sc, ci = lax.top_k(cand, PEER_TOPK)
        e1 = jnp.take_along_axis(i1, ci // PEER_TOPK, axis=-1)
        e2 = jnp.take_along_axis(i2, ci % PEER_TOPK, axis=-1)
        idx = e1 * N_KEYS + e2
        g = jax.nn.softmax(sc, axis=-1)
        act = jax.nn.gelu(jnp.einsum('thkd,td->thk', u_tab[idx], hb).astype(f32), approximate=False)
        return jnp.einsum('thk,thkd->td', (g * act).astype(hb.dtype), v_tab[idx])

    out = lax.map(block, h.reshape(-1, TOK_BLOCK, d))
    return out.reshape(bsz, t, d)


def setup_inputs(seed: int = 0) -> dict:
    key = jax.random.key(seed)
    ks = jax.random.split(key, 32)
    f32 = jnp.float32
    D = D_MODEL

    def nrm(k, shape, s):
        return jax.random.normal(k, shape, f32) * s

    return {
        'x': nrm(ks[0], (BATCH, SEQ, D), 1.0),
        'c': nrm(ks[1], (BATCH, D), 1.0),
        'ctx': nrm(ks[2], (BATCH, CTX_LEN, D), 1.0),
        'c_ctx': nrm(ks[3], (D,), 1.0),
        'ada_w': nrm(ks[4], (DEPTH, D, 6 * D), 0.5 * D ** -0.5),
        'ada_b': nrm(ks[5], (DEPTH, 6 * D), 0.02),
        'norm_gain': 1.0 + nrm(ks[6], (DEPTH, 2, D), 0.02),
        'w_in': nrm(ks[7], (DEPTH, D, IN_W), D ** -0.5),
        'conv_w': 1.0 / CONV_W + nrm(ks[8], (DEPTH, CONV_W, 3 * RW_WIDTH), 0.2),
        'rw_w0': jax.random.uniform(ks[9], (DEPTH, 2, RW_WIDTH), f32, -6.0, 1.0),
        'rw_w_up': nrm(ks[10], (DEPTH, 2, DECAY_LORA, RW_WIDTH), 0.1),
        'rw_a0': nrm(ks[11], (DEPTH, 2, RW_WIDTH), 0.5),
        'rw_a_up': nrm(ks[12], (DEPTH, 2, ICL_LORA, RW_WIDTH), 0.1),
        'rw_g_up': nrm(ks[13], (DEPTH, GATE_LORA, RW_WIDTH), GATE_LORA ** -0.5),
        'rw_k_k': 0.85 + nrm(ks[14], (DEPTH, RW_WIDTH), 0.05),
        'rw_k_a': 1.0 + nrm(ks[15], (DEPTH, RW_WIDTH), 0.05),
        'rw_r_k': nrm(ks[16], (DEPTH, RW_HEADS, RW_HEAD_DIM), 0.1),
        'rw_gn_gain': 1.0 + nrm(ks[17], (DEPTH, RW_WIDTH), 0.02),
        'rw_gn_bias': nrm(ks[18], (DEPTH, RW_WIDTH), 0.02),
        'da_q_gain': 1.0 + nrm(ks[19], (DEPTH, DA_HEAD_DIM), 0.02),
        'da_k_gain': 1.0 + nrm(ks[20], (DEPTH, DA_HEAD_DIM), 0.02),
        'da_lambda': nrm(ks[21], (DEPTH, 4, DA_HEAD_DIM), 0.1),
        'da_sub_gain': 1.0 + nrm(ks[22], (DEPTH, DA_V_DIM), 0.02),
        'w_out': nrm(ks[23], (DEPTH, MIX_WIDTH, D), MIX_WIDTH ** -0.5),
        'peer_wq': nrm(ks[24], (DEPTH, D, PEER_HEADS * PEER_QDIM), D ** -0.5),
        'peer_keys': nrm(ks[25], (DEPTH, PEER_HEADS, 2, N_KEYS, PEER_HALF), PEER_HALF ** -0.5),
        'peer_u': nrm(ks[26], (DEPTH, N_EXPERTS, D), D ** -0.5),
        'peer_v': nrm(ks[27], (DEPTH, N_EXPERTS, D), 0.5),
    }


def reference(x, c, ctx, c_ctx, ada_w, ada_b, norm_gain, w_in, conv_w,
              rw_w0, rw_w_up, rw_a0, rw_a_up, rw_g_up, rw_k_k, rw_k_a, rw_r_k,
              rw_gn_gain, rw_gn_bias, da_q_gain, da_k_gain, da_lambda, da_sub_gain,
              w_out, peer_wq, peer_keys, peer_u, peer_v):
    cos, sin = axial_rope_tables(x.shape[1])
    c_act = jax.nn.silu(c)
    cc_act = jax.nn.silu(c_ctx)
    xc = ctx
    for l in range(DEPTH):
        need_ctx = l < DEPTH - 1
        lam_init = 0.8 - 0.6 * math.exp(-0.3 * l)
        mod = jnp.split((c_act @ ada_w[l] + ada_b[l])[:, None, :], 6, axis=-1)
        mod_c = jnp.split((cc_act @ ada_w[l] + ada_b[l])[None, None, :], 6, axis=-1)

        h = modulate(rms_norm(x, norm_gain[l, 0]), mod[0], mod[1])
        hc = modulate(rms_norm(xc, norm_gain[l, 0]), mod_c[0], mod_c[1])
        lat_rw, lat_da = split_proj(h @ w_in[l], conv_w[l])
        ctx_rw, ctx_da = split_proj(hc @ w_in[l], conv_w[l])
        rw_lat, rw_ctx = rwkv7_group(lat_rw, ctx_rw, rw_w0[l], rw_w_up[l], rw_a0[l], rw_a_up[l],
                                     rw_g_up[l], rw_k_k[l], rw_k_a[l], rw_r_k[l],
                                     rw_gn_gain[l], rw_gn_bias[l], need_ctx)
        da_lat, da_ctx = diff_attention_group(lat_da, ctx_da, da_q_gain[l], da_k_gain[l],
                                              da_lambda[l], da_sub_gain[l], lam_init,
                                              cos, sin, need_ctx)
        x = x + mod[2] * (jnp.concatenate([rw_lat, da_lat], axis=-1) @ w_out[l])

        h = modulate(rms_norm(x, norm_gain[l, 1]), mod[3], mod[4])
        x = x + mod[5] * peer_ffn(h, peer_wq[l], peer_keys[l], peer_u[l], peer_v[l])

        if need_ctx:
            xc = xc + mod_c[2] * (jnp.concatenate([rw_ctx, da_ctx], axis=-1) @ w_out[l])
            hc = modulate(rms_norm(xc, norm_gain[l, 1]), mod_c[3], mod_c[4])
            xc = xc + mod_c[5] * peer_ffn(hc, peer_wq[l], peer_keys[l], peer_u[l], peer_v[l])
    return x
```

```python
import functools
import math

import jax
import jax.numpy as jnp
from jax import lax
from jax.experimental import pallas as pl
from jax.experimental.pallas import tpu as pltpu

F32 = jnp.float32
BF16 = jnp.bfloat16

HEAD_DIM = 64
LANES = 128
GRID_W = 64
ROPE_THETA = 10000.0
NORM_EPS = 1e-6
GN_EPS = 64e-5
L2_EPS = 1e-12
PEER_TOPK = 16
VMEM_LIMIT = 56 * 1024 * 1024


def _cparams(sem):
    return pltpu.CompilerParams(dimension_semantics=sem, vmem_limit_bytes=VMEM_LIMIT)


def _mm(a, b):
    return jnp.dot(a.astype(BF16), b.astype(BF16), preferred_element_type=F32)


def _segment_ones():
    r = lax.broadcasted_iota(jnp.int32, (LANES, LANES), 0) >> 6
    c = lax.broadcasted_iota(jnp.int32, (LANES, LANES), 1) >> 6
    return jnp.where(r == c, 1.0, 0.0).astype(BF16)


def _segsum(x, ones_bd):
    hi = x.astype(BF16)
    lo = (x - hi.astype(F32)).astype(BF16)
    return (jnp.dot(hi, ones_bd, preferred_element_type=F32)
            + jnp.dot(lo, ones_bd, preferred_element_type=F32))


def _tile(n, pref, align=8):
    for t in range(min(n, pref) // align * align, 0, -align):
        if n % t == 0:
            return t
    return n


def _mod_kernel(c_ref, w_ref, b_ref, o_ref):
    cv = c_ref[...]
    act = cv * jax.nn.sigmoid(cv)
    o_ref[...] = _mm(act, w_ref[...]) + b_ref[...]


def _modulation(c8, ada_w, ada_b):
    depth, d, n6 = ada_w.shape
    tn = _tile(n6, 1536)
    return pl.pallas_call(
        _mod_kernel,
        grid=(depth, n6 // tn),
        in_specs=[
            pl.BlockSpec((8, d), lambda l, j: (0, 0)),
            pl.BlockSpec((None, d, tn), lambda l, j: (l, 0, j)),
            pl.BlockSpec((None, 1, tn), lambda l, j: (l, 0, j)),
        ],
        out_specs=pl.BlockSpec((None, 8, tn), lambda l, j: (l, 0, j)),
        out_shape=jax.ShapeDtypeStruct((depth, 8, n6), F32),
        compiler_params=_cparams(("arbitrary", "arbitrary")),
        name="modulation",
    )(c8, ada_w, ada_b.reshape(depth, 1, n6))


def _norm_proj_kernel(gid_ref, x_ref, gain_ref, shift_ref, scale_ref, w_ref, o_ref, *rest, emit_h):
    del gid_ref
    hs_ref = rest[-1]

    @pl.when(pl.program_id(1) == 0)
    def _():
        x = x_ref[...]
        ms = jnp.mean(x * x, axis=-1, keepdims=True)
        y = x * lax.rsqrt(ms + NORM_EPS) * gain_ref[...]
        h = y * (1.0 + scale_ref[...]) + shift_ref[...]
        hs_ref[...] = h.astype(BF16)
        if emit_h:
            rest[0][...] = h.astype(BF16)

    o_ref[...] = jnp.dot(hs_ref[...], w_ref[...], preferred_element_type=F32)


def _norm_proj(x, gid, gain, shift, scale, w, tm, tn, emit_h=False):
    n, d = x.shape
    nout = w.shape[1]
    out_shape = [jax.ShapeDtypeStruct((n, nout), F32)]
    out_specs = [pl.BlockSpec((tm, tn), lambda i, j, g: (i, j))]
    if emit_h:
        out_shape.append(jax.ShapeDtypeStruct((n, d), BF16))
        out_specs.append(pl.BlockSpec((tm, d), lambda i, j, g: (i, 0)))
    res = pl.pallas_call(
        functools.partial(_norm_proj_kernel, emit_h=emit_h),
        grid_spec=pltpu.PrefetchScalarGridSpec(
            num_scalar_prefetch=1,
            grid=(n // tm, nout // tn),
            in_specs=[
                pl.BlockSpec((tm, d), lambda i, j, g: (i, 0)),
                pl.BlockSpec((1, d), lambda i, j, g: (0, 0)),
                pl.BlockSpec((None, 1, d), lambda i, j, g: (g[i], 0, 0)),
                pl.BlockSpec((None, 1, d), lambda i, j, g: (g[i], 0, 0)),
                pl.BlockSpec((d, tn), lambda i, j, g: (0, j)),
            ],
            out_specs=out_specs,
            scratch_shapes=[pltpu.VMEM((tm, d), BF16)],
        ),
        out_shape=out_shape,
        compiler_params=_cparams(("arbitrary", "arbitrary")),
        name="norm_proj",
    )(gid, x, gain.reshape(1, d), shift, scale, w)
    return res if emit_h else res[0]


def _gated_proj_kernel(gid_ref, y_ref, w_ref, x_ref, gate_ref, o_ref):
    del gid_ref
    o_ref[...] = x_ref[...] + gate_ref[...] * _mm(y_ref[...], w_ref[...])


def _gated_proj(y, w, x, gid, gate, tm):
    n, k = y.shape
    d = w.shape[1]
    return pl.pallas_call(
        _gated_proj_kernel,
        grid_spec=pltpu.PrefetchScalarGridSpec(
            num_scalar_prefetch=1,
            grid=(n // tm,),
            in_specs=[
                pl.BlockSpec((tm, k), lambda i, g: (i, 0)),
                pl.BlockSpec((k, d), lambda i, g: (0, 0)),
                pl.BlockSpec((tm, d), lambda i, g: (i, 0)),
                pl.BlockSpec((None, 1, d), lambda i, g: (g[i], 0, 0)),
            ],
            out_specs=pl.BlockSpec((tm, d), lambda i, g: (i, 0)),
        ),
        out_shape=jax.ShapeDtypeStruct((n, d), F32),
        compiler_params=_cparams(("arbitrary",)),
        name="gated_proj",
    )(gid, y, w, x, gate)


def _rwkv_prep_kernel(first_ref, last_ref, x_ref, prev_ref, next_ref, tail_ref, cw_ref,
                      w0_ref, wup_ref, a0_ref, aup_ref, gup_ref, kk_ref, ka_ref, rk_ref,
                      r_out, v_out, kap_out, w_out, k_out, b_out, bonus_out, gate_out,
                      *, rw, lw, la):
    i = pl.program_id(0)
    tm = x_ref.shape[0]
    x = x_ref[...]
    row = lax.broadcasted_iota(jnp.int32, x.shape, 0)
    keep_prev = jnp.where(first_ref[i] == 1, 0.0, 1.0)
    keep_next = jnp.where(last_ref[i] == 1, 0.0, 1.0)
    halo_prev = prev_ref[7:8, :] * keep_prev
    halo_next = next_ref[0:1, :] * keep_next
    x_prev = jnp.where(row == 0, halo_prev, pltpu.roll(x, 1, axis=0))
    x_next = jnp.where(row == tm - 1, halo_next, pltpu.roll(x, tm - 1, axis=0))
    conv = x_prev * cw_ref[0:1, :] + x * cw_ref[1:2, :] + x_next * cw_ref[2:3, :]
    r = conv[:, 0:rw]
    k = conv[:, rw:2 * rw]
    v = conv[:, 2 * rw:3 * rw]
    tail = tail_ref[...]
    w_lo = jnp.tanh(tail[:, 0:lw])
    a_lo = tail[:, lw:lw + la]
    g_lo = jax.nn.sigmoid(tail[:, lw + la:])
    ones_bd = _segment_ones()

    def seg(t):
        return jnp.concatenate(
            [_segsum(t[:, c * LANES:(c + 1) * LANES], ones_bd) for c in range(rw // LANES)], axis=1)

    kap = k * kk_ref[...]
    nrm = jnp.sqrt(seg(kap * kap))
    kap = kap / jnp.maximum(nrm, L2_EPS)
    r_out[...] = r
    v_out[...] = v
    kap_out[...] = kap
    gate_out[...] = _mm(g_lo, gup_ref[...])
    bonus = jnp.zeros_like(r)
    for d in range(2):
        z = w0_ref[d:d + 1, :] + _mm(w_lo, wup_ref[d])
        log_w = -jax.nn.softplus(-z) - 0.5
        w_out[d] = jnp.exp(-jnp.exp(log_w))
        a = jax.nn.sigmoid(a0_ref[d:d + 1, :] + _mm(a_lo, aup_ref[d]))
        k_mod = k * (1.0 + (a - 1.0) * ka_ref[...])
        k_out[d] = k_mod
        b_out[d] = kap * a
        bonus = bonus + seg(r * k_mod * rk_ref[...]) * v
    bonus_out[...] = bonus


def _rwkv_prep(p, first, last, conv_w, w0, w_up, a0, a_up, g_up, k_k, k_a, r_k, tm):
    n = p.shape[0]
    rw = k_k.shape[0]
    lw, la, lg = w_up.shape[1], a_up.shape[1], g_up.shape[0]
    tail_w = lw + la + lg
    tail_blk = (p.shape[1] - tail_w) // tail_w
    nb8 = n // 8
    row2 = lambda i, f, l: (0, 0)
    row3 = lambda i, f, l: (0, 0, 0)
    one = jax.ShapeDtypeStruct((n, rw), F32)
    two = jax.ShapeDtypeStruct((2, n, rw), F32)
    one_spec = pl.BlockSpec((tm, rw), lambda i, f, l: (i, 0))
    two_spec = pl.BlockSpec((2, tm, rw), lambda i, f, l: (0, i, 0))
    return pl.pallas_call(
        functools.partial(_rwkv_prep_kernel, rw=rw, lw=lw, la=la),
        grid_spec=pltpu.PrefetchScalarGridSpec(
            num_scalar_prefetch=2,
            grid=(n // tm,),
            in_specs=[
                pl.BlockSpec((tm, 3 * rw), lambda i, f, l: (i, 0)),
                pl.BlockSpec((8, 3 * rw), lambda i, f, l: (jnp.maximum(i * (tm // 8) - 1, 0), 0)),
                pl.BlockSpec((8, 3 * rw), lambda i, f, l: (jnp.minimum((i + 1) * (tm // 8), nb8 - 1), 0)),
                pl.BlockSpec((tm, tail_w), lambda i, f, l: (i, tail_blk)),
                pl.BlockSpec((3, 3 * rw), row2),
                pl.BlockSpec((2, rw), row2),
                pl.BlockSpec((2, lw, rw), row3),
                pl.BlockSpec((2, rw), row2),
                pl.BlockSpec((2, la, rw), row3),
                pl.BlockSpec((lg, rw), row2),
                pl.BlockSpec((1, rw), row2),
                pl.BlockSpec((1, rw), row2),
                pl.BlockSpec((1, rw), row2),
            ],
            out_specs=[one_spec, one_spec, one_spec, two_spec, two_spec, two_spec, one_spec, one_spec],
        ),
        out_shape=[one, one, one, two, two, two, one, one],
        compiler_params=_cparams(("arbitrary",)),
        name="rwkv_prep",
    )(first, last, p, p, p, p, conv_w, w0, w_up, a0, a_up, g_up,
      k_k.reshape(1, rw), k_a.reshape(1, rw), r_k.reshape(1, rw))


def _rwkv_scan_kernel(r_ref, w_ref, k_ref, v_ref, kap_ref, b_ref, y_ref, s_ref):
    tc, hd, ns = r_ref.shape

    @pl.when(pl.program_id(0) == 0)
    def _():
        s_ref[...] = jnp.zeros_like(s_ref)

    def step(t, carry):
        def row(ref, kk):
            return ref[t, pl.ds(kk, 1), :]

        sa = jnp.zeros((hd, ns), F32)
        for kk in range(hd):
            sa = sa + s_ref[kk] * row(kap_ref, kk)
        vv = v_ref[t]
        y = jnp.zeros((hd, ns), F32)
        for kk in range(hd):
            s_new = s_ref[kk] * row(w_ref, kk) + (vv * row(k_ref, kk) - sa * row(b_ref, kk))
            s_ref[kk] = s_new
            y = y + s_new * row(r_ref, kk)
        y_ref[t] = y
        return carry

    lax.fori_loop(0, tc, step, 0)


def _rwkv_scan(r, w, k, v, kap, b, tc):
    steps, hd, ns = r.shape
    spec = pl.BlockSpec((tc, hd, ns), lambda i: (i, 0, 0))
    return pl.pallas_call(
        _rwkv_scan_kernel,
        grid=(steps // tc,),
        in_specs=[spec] * 6,
        out_specs=spec,
        out_shape=jax.ShapeDtypeStruct((steps, hd, ns), F32),
        scratch_shapes=[pltpu.VMEM((hd, hd, ns), F32)],
        compiler_params=_cparams(("arbitrary",)),
        name="rwkv_scan",
    )(r, w, k, v, kap, b)


def _rwkv_finish_kernel(y_ref, bonus_ref, gate_ref, gain_ref, bias_ref, o_ref):
    ones_bd = _segment_ones()
    y = y_ref[...]
    outs = []
    for c in range(y.shape[1] // LANES):
        yc = y[:, c * LANES:(c + 1) * LANES]
        mu = _segsum(yc, ones_bd) * (1.0 / HEAD_DIM)
        dev = yc - mu
        var = _segsum(dev * dev, ones_bd) * (1.0 / HEAD_DIM)
        outs.append(dev * lax.rsqrt(var + GN_EPS))
    yn = jnp.concatenate(outs, axis=1)
    o_ref[...] = ((yn * gain_ref[...] + bias_ref[...] + bonus_ref[...]) * gate_ref[...]).astype(o_ref.dtype)


def _rwkv_finish(y, bonus, gate, gain, bias, tm):
    n, rw = y.shape
    spec = pl.BlockSpec((tm, rw), lambda i: (i, 0))
    vec = pl.BlockSpec((1, rw), lambda i: (0, 0))
    return pl.pallas_call(
        _rwkv_finish_kernel,
        grid=(n // tm,),
        in_specs=[spec, spec, spec, vec, vec],
        out_specs=spec,
        out_shape=jax.ShapeDtypeStruct((n, rw), BF16),
        compiler_params=_cparams(("arbitrary",)),
        name="rwkv_finish",
    )(y, bonus, gate, gain.reshape(1, rw), bias.reshape(1, rw))


def _da_prep_kernel(x_ref, cos_ref, sin_ref, qg_ref, kg_ref, o_ref):
    s = pl.program_id(1)

    @pl.when(s == 2)
    def _():
        o_ref[...] = x_ref[...].astype(BF16)

    @pl.when(s < 2)
    def _():
        ones_bd = _segment_ones()
        gain = jnp.where(s == 0, qg_ref[...] * (HEAD_DIM ** -0.5), kg_ref[...])
        cos = cos_ref[...]
        sin = sin_ref[...]
        lane = lax.broadcasted_iota(jnp.int32, cos.shape, 1)
        even = ((lane >> 4) & 1) == 0
        x = x_ref[...]
        for c in range(x.shape[1] // LANES):
            xc = x[:, c * LANES:(c + 1) * LANES]
            ms = _segsum(xc * xc, ones_bd) * (1.0 / HEAD_DIM)
            xn = xc * lax.rsqrt(ms + NORM_EPS) * gain
            swapped = jnp.where(even, pltpu.roll(xn, LANES - HEAD_DIM // 4, axis=1),
                                pltpu.roll(xn, HEAD_DIM // 4, axis=1))
            o_ref[:, c * LANES:(c + 1) * LANES] = (xn * cos + swapped * sin).astype(BF16)


def _da_prep(p, cos_t, sin_t, q_gain, k_gain, col0, width, tm):
    n = p.shape[0]
    blk0 = col0 // width
    qg = jnp.tile(q_gain, LANES // HEAD_DIM).reshape(1, LANES)
    kg = jnp.tile(k_gain, LANES // HEAD_DIM).reshape(1, LANES)
    return pl.pallas_call(
        _da_prep_kernel,
        grid=(n // tm, 3),
        in_specs=[
            pl.BlockSpec((tm, width), lambda i, s: (i, blk0 + s)),
            pl.BlockSpec((tm, LANES), lambda i, s: (i, 0)),
            pl.BlockSpec((tm, LANES), lambda i, s: (i, 0)),
            pl.BlockSpec((1, LANES), lambda i, s: (0, 0)),
            pl.BlockSpec((1, LANES), lambda i, s: (0, 0)),
        ],
        out_specs=pl.BlockSpec((None, tm, width), lambda i, s: (s, i, 0)),
        out_shape=jax.ShapeDtypeStruct((3, n, width), BF16),
        compiler_params=_cparams(("arbitrary", "arbitrary")),
        name="da_prep",
    )(p, cos_t, sin_t, qg, kg)


def _diff_attn_kernel(*refs, lam_init, two_sets):
    if two_sets:
        q_ref, k_ref, v_ref, kc_ref, vc_ref, lam_ref, sg_ref, o_ref = refs
    else:
        q_ref, k_ref, v_ref, lam_ref, sg_ref, o_ref = refs
    lv = lam_ref[...]
    lam = (jnp.exp(jnp.sum(lv[0:1] * lv[1:2], axis=1, keepdims=True))
           - jnp.exp(jnp.sum(lv[2:3] * lv[3:4], axis=1, keepdims=True)) + lam_init)
    q = q_ref[...]
    lane = lax.broadcasted_iota(jnp.int32, q.shape, 1)
    dims = (((1,), (1,)), ((), ()))
    zero = jnp.zeros_like(q)
    outs = []
    for m in range(2):
        qm = jnp.where((lane >> 6) == m, q, zero)
        s = lax.dot_general(qm, k_ref[...], dims, preferred_element_type=F32)
        mx = jnp.max(s, axis=1, keepdims=True)
        if two_sets:
            sc = lax.dot_general(qm, kc_ref[...], dims, preferred_element_type=F32)
            mx = jnp.maximum(mx, jnp.max(sc, axis=1, keepdims=True))
        p = jnp.exp(s - mx)
        den = jnp.sum(p, axis=1, keepdims=True)
        o = jnp.dot(p.astype(BF16), v_ref[...], preferred_element_type=F32)
        if two_sets:
            pc = jnp.exp(sc - mx)
            den = den + jnp.sum(pc, axis=1, keepdims=True)
            o = o + jnp.dot(pc.astype(BF16), vc_ref[...], preferred_element_type=F32)
        outs.append(o / den)
    o = outs[0] - lam * outs[1]
    ms = jnp.mean(o * o, axis=1, keepdims=True)
    o_ref[...] = (o * lax.rsqrt(ms + NORM_EPS) * sg_ref[...] * (1.0 - lam_init)).astype(o_ref.dtype)


def _diff_attention(qkv, lam_vec, sub_gain, lam_init, batch, t_lat, t_ctx, tq):
    _, n, width = qkv.shape
    heads = width // LANES
    n_lat = batch * t_lat
    sg = sub_gain.reshape(1, LANES)
    lat_q = t_lat // tq
    ctx_blk = n_lat // t_ctx
    lat = pl.pallas_call(
        functools.partial(_diff_attn_kernel, lam_init=lam_init, two_sets=True),
        grid=(batch, heads, lat_q),
        in_specs=[
            pl.BlockSpec((None, tq, LANES), lambda b, h, i: (0, b * lat_q + i, h)),
            pl.BlockSpec((None, t_lat, LANES), lambda b, h, i: (1, b, h)),
            pl.BlockSpec((None, t_lat, LANES), lambda b, h, i: (2, b, h)),
            pl.BlockSpec((None, t_ctx, LANES), lambda b, h, i: (1, ctx_blk + b, h)),
            pl.BlockSpec((None, t_ctx, LANES), lambda b, h, i: (2, ctx_blk + b, h)),
            pl.BlockSpec((4, HEAD_DIM), lambda b, h, i: (0, 0)),
            pl.BlockSpec((1, LANES), lambda b, h, i: (0, 0)),
        ],
        out_specs=pl.BlockSpec((tq, LANES), lambda b, h, i: (b * lat_q + i, h)),
        out_shape=jax.ShapeDtypeStruct((n_lat, width), BF16),
        compiler_params=_cparams(("arbitrary", "arbitrary", "arbitrary")),
        name="diff_attn_lat",
    )(qkv, qkv, qkv, qkv, qkv, lam_vec, sg)
    ctx = pl.pallas_call(
        functools.partial(_diff_attn_kernel, lam_init=lam_init, two_sets=False),
        grid=(batch, heads),
        in_specs=[
            pl.BlockSpec((None, t_ctx, LANES), lambda b, h: (0, ctx_blk + b, h)),
            pl.BlockSpec((None, t_ctx, LANES), lambda b, h: (1, ctx_blk + b, h)),
            pl.BlockSpec((None, t_ctx, LANES), lambda b, h: (2, ctx_blk + b, h)),
            pl.BlockSpec((4, HEAD_DIM), lambda b, h: (0, 0)),
            pl.BlockSpec((1, LANES), lambda b, h: (0, 0)),
        ],
        out_specs=pl.BlockSpec((t_ctx, LANES), lambda b, h: (b, h)),
        out_shape=jax.ShapeDtypeStruct((batch * t_ctx, width), BF16),
        compiler_params=_cparams(("arbitrary", "arbitrary")),
        name="diff_attn_ctx",
    )(qkv, qkv, qkv, lam_vec, sg)
    return jnp.concatenate([lat, ctx], axis=0)


def _peer_scores_kernel(q_ref, keys_ref, o_ref):
    o_ref[...] = lax.dot_general(keys_ref[...], q_ref[...], (((1,), (1,)), ((), ())),
                                 precision=lax.Precision.HIGHEST, preferred_element_type=F32)


def _peer_scores(q, keys, tt):
    n = q.shape[0]
    hp, nk, half = keys.shape
    return pl.pallas_call(
        _peer_scores_kernel,
        grid=(n // tt, hp),
        in_specs=[
            pl.BlockSpec((tt, half), lambda i, j: (i, j)),
            pl.BlockSpec((None, nk, half), lambda i, j: (j, 0, 0)),
        ],
        out_specs=pl.BlockSpec((None, nk, tt), lambda i, j: (j, 0, i)),
        out_shape=jax.ShapeDtypeStruct((hp, nk, n), F32),
        compiler_params=_cparams(("arbitrary", "arbitrary")),
        name="peer_scores",
    )(q, keys)


def _top_candidates():
    return [(a, b) for a in range(PEER_TOPK) for b in range(PEER_TOPK) if (a + 1) * (b + 1) <= PEER_TOPK]


def _peer_select_kernel(s_ref, e1_ref, bi_ref, e2_ref, r2_ref):
    nk = s_ref.shape[1]
    neg = -jnp.inf
    unranked = float(2 * PEER_TOPK)

    def top(s):
        key = lax.broadcasted_iota(jnp.int32, s.shape, 0)
        work = s
        rank = jnp.full(s.shape, unranked, F32)
        vals = []
        for r in range(PEER_TOPK):
            m = jnp.max(work, axis=0, keepdims=True)
            idx = jnp.min(jnp.where(work == m, key, nk), axis=0, keepdims=True)
            hit = key == idx
            work = jnp.where(hit, neg, work)
            rank = jnp.where(hit, float(r), rank)
            vals.append(m[0])
        return vals, rank

    s1 = s_ref[0]
    s2 = s_ref[1]
    v1, rank1 = top(s1)
    v2, rank2 = top(s2)
    pairs = _top_candidates()
    work = [v1[a] + v2[b] for a, b in pairs]
    flat = [a * PEER_TOPK + b for a, b in pairs]
    top_sum = work[0]
    count = [jnp.zeros_like(top_sum) for _ in range(PEER_TOPK)]
    z = jnp.zeros_like(top_sum)
    big = PEER_TOPK * PEER_TOPK
    for _ in range(PEER_TOPK):
        m = functools.reduce(jnp.maximum, work)
        idx = functools.reduce(jnp.minimum, [jnp.where(w == m, f, big) for w, f in zip(work, flat)])
        z = z + jnp.exp(m - top_sum)
        for n, (a, _) in enumerate(pairs):
            hit = idx == flat[n]
            work[n] = jnp.where(hit, neg, work[n])
            count[a] = count[a] + jnp.where(hit, 1.0, 0.0)
    bi = jnp.zeros(s1.shape, F32)
    for a in range(PEER_TOPK):
        bi = jnp.where(rank1 == float(a), count[a][None], bi)
    bi_ref[...] = bi
    e1_ref[...] = jnp.exp(s1 - v1[0][None]) / z[None]
    e2_ref[...] = jnp.exp(s2 - v2[0][None])
    r2_ref[...] = rank2


def _peer_select(scores, heads):
    hp, nk, groups, lanes = scores.shape
    g8 = min(groups, 8)
    out = jax.ShapeDtypeStruct((heads, nk, groups, lanes), F32)
    spec = pl.BlockSpec((None, nk, g8, lanes), lambda g, h: (h, 0, g, 0))
    return pl.pallas_call(
        _peer_select_kernel,
        grid=(groups // g8, heads),
        in_specs=[pl.BlockSpec((None, 2, nk, g8, lanes), lambda g, h: (h, 0, 0, g, 0))],
        out_specs=[spec] * 4,
        out_shape=[out] * 4,
        compiler_params=_cparams(("arbitrary", "arbitrary")),
        name="peer_select",
    )(scores.reshape(heads, 2, nk, groups, lanes))


def _peer_dense_kernel(h_ref, u_ref, vt_ref, e1_ref, bi_ref, e2_ref, r2_ref, o_ref, *, nk):
    e = pl.program_id(1)
    heads = e1_ref.shape[0]
    et = u_ref.shape[0]

    @pl.when(e == 0)
    def _():
        o_ref[...] = jnp.zeros_like(o_ref)

    a = lax.dot_general(u_ref[...], h_ref[...], (((1,), (1,)), ((), ())), preferred_element_type=F32)
    act = 0.5 * a * (1.0 + lax.erf(a * (2.0 ** -0.5)))
    parts = []
    for ii in range(et // nk):
        i = e * (et // nk) + ii
        g = jnp.zeros((nk, a.shape[1]), F32)
        for hh in range(heads):
            e1 = e1_ref[hh, pl.ds(i, 1), :]
            bi = bi_ref[hh, pl.ds(i, 1), :]
            g = g + jnp.where(r2_ref[hh] < bi, e2_ref[hh] * e1, 0.0)
        parts.append((g * act[ii * nk:(ii + 1) * nk]).astype(BF16))
    m = jnp.concatenate(parts, axis=0) if len(parts) > 1 else parts[0]
    o_ref[...] += jnp.dot(vt_ref[...], m, preferred_element_type=F32)


def _peer_dense(h, u, vt, e1, bi, e2, r2, tl, et):
    n, d = h.shape
    ne = u.shape[0]
    heads, nk, _ = e1.shape
    fac = pl.BlockSpec((heads, nk, tl), lambda t, e: (0, 0, t))
    return pl.pallas_call(
        functools.partial(_peer_dense_kernel, nk=nk),
        grid=(n // tl, ne // et),
        in_specs=[
            pl.BlockSpec((tl, d), lambda t, e: (t, 0)),
            pl.BlockSpec((et, d), lambda t, e: (e, 0)),
            pl.BlockSpec((d, et), lambda t, e: (0, e)),
            fac, fac, fac, fac,
        ],
        out_specs=pl.BlockSpec((d, tl), lambda t, e: (0, t)),
        out_shape=jax.ShapeDtypeStruct((d, n), F32),
        compiler_params=_cparams(("arbitrary", "arbitrary")),
        name="peer_dense",
    )(h, u, vt, e1, bi, e2, r2)


def _gated_add_kernel(gid_ref, x_ref, y_ref, gate_ref, o_ref):
    del gid_ref
    o_ref[...] = x_ref[...] + gate_ref[...] * y_ref[...]


def _gated_add(x, y, gid, gate, tm):
    n, d = x.shape
    spec = pl.BlockSpec((tm, d), lambda i, g: (i, 0))
    return pl.pallas_call(
        _gated_add_kernel,
        grid_spec=pltpu.PrefetchScalarGridSpec(
            num_scalar_prefetch=1,
            grid=(n // tm,),
            in_specs=[spec, spec, pl.BlockSpec((None, 1, d), lambda i, g: (g[i], 0, 0))],
            out_specs=spec,
        ),
        out_shape=jax.ShapeDtypeStruct((n, d), F32),
        compiler_params=_cparams(("arbitrary",)),
        name="gated_add",
    )(gid, x, y, gate)


def _rope_tables(t_lat, n_lat, n):
    rows = t_lat // GRID_W
    row = jnp.repeat(jnp.arange(rows, dtype=jnp.int32), GRID_W)
    col = jnp.tile(jnp.arange(GRID_W, dtype=jnp.int32), rows)
    nfreq = HEAD_DIM // 4
    inv_freq = ROPE_THETA ** (-jnp.arange(nfreq, dtype=F32) / nfreq)
    ang_r = row.astype(F32)[:, None] * inv_freq
    ang_c = col.astype(F32)[:, None] * inv_freq
    cos = jnp.concatenate([jnp.cos(ang_r)] * 2 + [jnp.cos(ang_c)] * 2, axis=1)
    sin = jnp.concatenate([-jnp.sin(ang_r), jnp.sin(ang_r), -jnp.sin(ang_c), jnp.sin(ang_c)], axis=1)
    cos = jnp.tile(cos, (n_lat // t_lat, LANES // HEAD_DIM))
    sin = jnp.tile(sin, (n_lat // t_lat, LANES // HEAD_DIM))
    pad = n - n_lat
    cos = jnp.concatenate([cos, jnp.ones((pad, LANES), F32)], axis=0)
    sin = jnp.concatenate([sin, jnp.zeros((pad, LANES), F32)], axis=0)
    return cos, sin


def _to_streams(arr, batch, t_lat, t_ctx):
    rw = arr.shape[-1]
    heads = rw // HEAD_DIM
    n_lat = batch * t_lat
    lat = arr[:, :n_lat].reshape(2, batch, t_lat, heads, HEAD_DIM)
    ctx = arr[:, n_lat:].reshape(2, batch, t_ctx, heads, HEAD_DIM)
    fwd = jnp.concatenate([ctx[0], lat[0]], axis=1)
    bwd = jnp.concatenate([ctx[1, :, ::-1], lat[1, :, ::-1]], axis=1)
    seq = jnp.stack([fwd, bwd], axis=0)
    return jnp.transpose(seq, (2, 4, 0, 1, 3)).reshape(t_ctx + t_lat, HEAD_DIM, 2 * batch * heads)


def _from_streams(y, batch, t_lat, t_ctx, heads):
    steps = y.shape[0]
    seq = jnp.transpose(y.reshape(steps, HEAD_DIM, 2, batch, heads), (2, 3, 0, 4, 1))
    seq = seq.reshape(2, batch, steps, heads * HEAD_DIM)
    ctx = seq[0, :, :t_ctx] + seq[1, :, :t_ctx][:, ::-1]
    lat = seq[0, :, t_ctx:] + seq[1, :, t_ctx:][:, ::-1]
    return jnp.concatenate([lat.reshape(batch * t_lat, -1), ctx.reshape(batch * t_ctx, -1)], axis=0)


def kernel(x, c, ctx, c_ctx, ada_w, ada_b, norm_gain, w_in, conv_w, rw_w0, rw_w_up, rw_a0, rw_a_up,
           rw_g_up, rw_k_k, rw_k_a, rw_r_k, rw_gn_gain, rw_gn_bias, da_q_gain, da_k_gain, da_lambda,
           da_sub_gain, w_out, peer_wq, peer_keys, peer_u, peer_v):
    batch, t_lat, d = x.shape
    t_ctx = ctx.shape[1]
    depth = ada_w.shape[0]
    rw = rw_k_k.shape[1]
    rw_heads = rw // HEAD_DIM
    lora_w = rw_w_up.shape[2] + rw_a_up.shape[2] + rw_g_up.shape[1]
    da_w = w_out.shape[1] - rw
    n_lat = batch * t_lat
    n = n_lat + batch * t_ctx
    p_heads, _, nk, half = peer_keys.shape[1:]

    tm = _tile(math.gcd(t_lat, batch * t_ctx), 512)
    ts = _tile(math.gcd(t_lat, t_ctx), 256)
    gid = jnp.minimum(jnp.arange(n // tm, dtype=jnp.int32) * tm // t_lat, batch)
    seq_len = jnp.where(jnp.arange(n // ts) * ts < n_lat, t_lat, t_ctx)
    seq_pos = jnp.where(jnp.arange(n // ts) * ts < n_lat, (jnp.arange(n // ts) * ts) % t_lat,
                        (jnp.arange(n // ts) * ts - n_lat) % t_ctx)
    first = (seq_pos == 0).astype(jnp.int32)
    last = (seq_pos + ts == seq_len).astype(jnp.int32)

    xs = jnp.concatenate([x.reshape(n_lat, d), ctx.reshape(batch * t_ctx, d)], axis=0)
    c8 = jnp.concatenate([c, c_ctx[None], jnp.zeros((8 - batch - 1, d), F32)], axis=0)
    mods = _modulation(c8, ada_w, ada_b)[:, :batch + 1].reshape(depth, batch + 1, 6, 1, d)
    cos_t, sin_t = _rope_tables(t_lat, n_lat, n)

    cols = jnp.concatenate([jnp.arange(3 * rw), jnp.arange(3 * rw + lora_w, w_in.shape[2]),
                            jnp.arange(3 * rw, 3 * rw + lora_w)])
    tn_in = _tile(w_in.shape[2], 1280)

    for l in range(depth):
        lam_init = 0.8 - 0.6 * math.exp(-0.3 * l)
        mod = [mods[l, :, j] for j in range(6)]

        p = _norm_proj(xs, gid, norm_gain[l, 0], mod[0], mod[1], w_in[l][:, cols].astype(BF16), tm, tn_in)
        r, v, kap, w2, k2, b2, bonus, gate = _rwkv_prep(
            p, first, last, conv_w[l], rw_w0[l], rw_w_up[l], rw_a0[l], rw_a_up[l], rw_g_up[l],
            rw_k_k[l], rw_k_a[l], rw_r_k[l].reshape(-1), ts)
        both = lambda a: jnp.stack([a, a], axis=0)
        streams = [_to_streams(a, batch, t_lat, t_ctx) for a in (both(r), w2, k2, both(v), both(kap), b2)]
        y = _rwkv_scan(*streams, tc=_tile(t_lat + t_ctx, 32))
        y = _from_streams(y, batch, t_lat, t_ctx, rw_heads)
        rw_out = _rwkv_finish(y, bonus, gate, rw_gn_gain[l], rw_gn_bias[l], ts)

        qkv = _da_prep(p, cos_t, sin_t, da_q_gain[l], da_k_gain[l], 3 * rw, da_w, ts)
        da_out = _diff_attention(qkv, da_lambda[l], da_sub_gain[l], lam_init, batch, t_lat, t_ctx,
                                 _tile(t_lat, 256))
        mix = jnp.concatenate([rw_out, da_out], axis=1)
        xs = _gated_proj(mix, w_out[l].astype(BF16), xs, gid, mod[2], tm)

        q, h = _norm_proj(xs, gid, norm_gain[l, 1], mod[3], mod[4], peer_wq[l].astype(BF16), tm,
                          _tile(peer_wq.shape[2], 2048), emit_h=True)
        scores = _peer_scores(q, peer_keys[l].reshape(2 * p_heads, nk, half), _tile(n, 1024))
        e1, bi, e2, r2 = _peer_select(scores.reshape(2 * p_heads, nk, n // LANES, LANES), p_heads)
        fac = [a.reshape(p_heads, nk, n) for a in (e1, bi, e2, r2)]
        out_t = _peer_dense(h, peer_u[l].astype(BF16), peer_v[l].T.astype(BF16), *fac,
                            tl=_tile(n, 512), et=2 * nk)
        xs = _gated_add(xs, out_t.T, gid, mod[5], tm)

    return xs[:n_lat].reshape(batch, t_lat, d)
```

```python
import functools
import math

import jax
import jax.numpy as jnp
from jax import lax
from jax.experimental import pallas as pl
from jax.experimental.pallas import tpu as pltpu

F32 = jnp.float32
BF16 = jnp.bfloat16

HEAD_DIM = 64
LANES = 128
GRID_W = 64
ROPE_THETA = 10000.0
NORM_EPS = 1e-6
GN_EPS = 64e-5
L2_EPS = 1e-12
PEER_TOPK = 16
VMEM_LIMIT = 56 * 1024 * 1024


def _cparams(sem):
    return pltpu.CompilerParams(dimension_semantics=sem, vmem_limit_bytes=VMEM_LIMIT)


def _mm(a, b):
    return jnp.dot(a.astype(BF16), b.astype(BF16), preferred_element_type=F32)


def _segment_ones():
    r = lax.broadcasted_iota(jnp.int32, (LANES, LANES), 0) >> 6
    c = lax.broadcasted_iota(jnp.int32, (LANES, LANES), 1) >> 6
    return jnp.where(r == c, 1.0, 0.0).astype(BF16)


def _segsum(x, ones_bd):
    hi = x.astype(BF16)
    lo = (x - hi.astype(F32)).astype(BF16)
    return (jnp.dot(hi, ones_bd, preferred_element_type=F32)
            + jnp.dot(lo, ones_bd, preferred_element_type=F32))


def _tile(n, pref, align=8):
    for t in range(min(n, pref) // align * align, 0, -align):
        if n % t == 0:
            return t
    return n


def _mod_kernel(c_ref, w_ref, b_ref, o_ref):
    cv = c_ref[...]
    act = cv * jax.nn.sigmoid(cv)
    o_ref[...] = _mm(act, w_ref[...]) + b_ref[...]


def _modulation(c8, ada_w, ada_b):
    depth, d, n6 = ada_w.shape
    tn = _tile(n6, 1536)
    return pl.pallas_call(
        _mod_kernel,
        grid=(depth, n6 // tn),
        in_specs=[
            pl.BlockSpec((8, d), lambda l, j: (0, 0)),
            pl.BlockSpec((None, d, tn), lambda l, j: (l, 0, j)),
            pl.BlockSpec((None, 1, tn), lambda l, j: (l, 0, j)),
        ],
        out_specs=pl.BlockSpec((None, 8, tn), lambda l, j: (l, 0, j)),
        out_shape=jax.ShapeDtypeStruct((depth, 8, n6), F32),
        compiler_params=_cparams(("arbitrary", "arbitrary")),
        name="modulation",
    )(c8, ada_w, ada_b.reshape(depth, 1, n6))


def _norm_proj_kernel(gid_ref, x_ref, gain_ref, shift_ref, scale_ref, w_ref, o_ref, *rest, emit_h):
    del gid_ref
    hs_ref = rest[-1]

    @pl.when(pl.program_id(1) == 0)
    def _():
        x = x_ref[...]
        ms = jnp.mean(x * x, axis=-1, keepdims=True)
        y = x * lax.rsqrt(ms + NORM_EPS) * gain_ref[...]
        h = y * (1.0 + scale_ref[...]) + shift_ref[...]
        hs_ref[...] = h.astype(BF16)
        if emit_h:
            rest[0][...] = h.astype(BF16)

    o_ref[...] = jnp.dot(hs_ref[...], w_ref[...], preferred_element_type=F32)


def _norm_proj(x, gid, gain, shift, scale, w, tm, tn, emit_h=False):
    n, d = x.shape
    nout = w.shape[1]
    out_shape = [jax.ShapeDtypeStruct((n, nout), F32)]
    out_specs = [pl.BlockSpec((tm, tn), lambda i, j, g: (i, j))]
    if emit_h:
        out_shape.append(jax.ShapeDtypeStruct((n, d), BF16))
        out_specs.append(pl.BlockSpec((tm, d), lambda i, j, g: (i, 0)))
    res = pl.pallas_call(
        functools.partial(_norm_proj_kernel, emit_h=emit_h),
        grid_spec=pltpu.PrefetchScalarGridSpec(
            num_scalar_prefetch=1,
            grid=(n // tm, nout // tn),
            in_specs=[
                pl.BlockSpec((tm, d), lambda i, j, g: (i, 0)),
                pl.BlockSpec((1, d), lambda i, j, g: (0, 0)),
                pl.BlockSpec((None, 1, d), lambda i, j, g: (g[i], 0, 0)),
                pl.BlockSpec((None, 1, d), lambda i, j, g: (g[i], 0, 0)),
                pl.BlockSpec((d, tn), lambda i, j, g: (0, j)),
            ],
            out_specs=out_specs,
            scratch_shapes=[pltpu.VMEM((tm, d), BF16)],
        ),
        out_shape=out_shape,
        compiler_params=_cparams(("arbitrary", "arbitrary")),
        name="norm_proj",
    )(gid, x, gain.reshape(1, d), shift, scale, w)
    return res if emit_h else res[0]


def _gated_proj_kernel(gid_ref, y1_ref, y2_ref, w1_ref, w2_ref, x_ref, gate_ref, o_ref):
    del gid_ref
    mix = (jnp.dot(y1_ref[...], w1_ref[...], preferred_element_type=F32)
           + jnp.dot(y2_ref[...], w2_ref[...], preferred_element_type=F32))
    o_ref[...] = x_ref[...] + gate_ref[...] * mix


def _gated_proj(y1, y2, w, x, gid, gate, tm):
    n, k1 = y1.shape
    k2 = y2.shape[1]
    d = w.shape[1]
    return pl.pallas_call(
        _gated_proj_kernel,
        grid_spec=pltpu.PrefetchScalarGridSpec(
            num_scalar_prefetch=1,
            grid=(n // tm,),
            in_specs=[
                pl.BlockSpec((tm, k1), lambda i, g: (i, 0)),
                pl.BlockSpec((tm, k2), lambda i, g: (i, 0)),
                pl.BlockSpec((k1, d), lambda i, g: (0, 0)),
                pl.BlockSpec((k2, d), lambda i, g: (k1 // k2, 0)),
                pl.BlockSpec((tm, d), lambda i, g: (i, 0)),
                pl.BlockSpec((None, 1, d), lambda i, g: (g[i], 0, 0)),
            ],
            out_specs=pl.BlockSpec((tm, d), lambda i, g: (i, 0)),
        ),
        out_shape=jax.ShapeDtypeStruct((n, d), F32),
        compiler_params=_cparams(("arbitrary",)),
        name="gated_proj",
    )(gid, y1, y2, w, w, x, gate)


def _rwkv_prep_kernel(first_ref, last_ref, x_ref, prev_ref, next_ref, tail_ref, cw_ref,
                      w0_ref, wup_ref, a0_ref, aup_ref, gup_ref, kk_ref, ka_ref, rk_ref,
                      q_out, v_out, bonus_out, gate_out, *, rw, lw, la):
    i = pl.program_id(0)
    tm = x_ref.shape[0]
    x = x_ref[...]
    row = lax.broadcasted_iota(jnp.int32, x.shape, 0)
    keep_prev = jnp.where(first_ref[i] == 1, 0.0, 1.0)
    keep_next = jnp.where(last_ref[i] == 1, 0.0, 1.0)
    halo_prev = prev_ref[7:8, :] * keep_prev
    halo_next = next_ref[0:1, :] * keep_next
    x_prev = jnp.where(row == 0, halo_prev, pltpu.roll(x, 1, axis=0))
    x_next = jnp.where(row == tm - 1, halo_next, pltpu.roll(x, tm - 1, axis=0))
    conv = x_prev * cw_ref[0:1, :] + x * cw_ref[1:2, :] + x_next * cw_ref[2:3, :]
    r = conv[:, 0:rw]
    k = conv[:, rw:2 * rw]
    v = conv[:, 2 * rw:3 * rw]
    tail = tail_ref[...]
    w_lo = jnp.tanh(tail[:, 0:lw])
    a_lo = tail[:, lw:lw + la]
    g_lo = jax.nn.sigmoid(tail[:, lw + la:])
    ones_bd = _segment_ones()

    def seg(t):
        return jnp.concatenate(
            [_segsum(t[:, c * LANES:(c + 1) * LANES], ones_bd) for c in range(rw // LANES)], axis=1)

    kap = k * kk_ref[...]
    nrm = jnp.sqrt(seg(kap * kap))
    kap = kap / jnp.maximum(nrm, L2_EPS)
    q_out[0] = r
    q_out[1] = kap
    v_out[...] = v
    gate_out[...] = _mm(g_lo, gup_ref[...])
    bonus = jnp.zeros_like(r)
    for d in range(2):
        z = w0_ref[d:d + 1, :] + _mm(w_lo, wup_ref[d])
        log_w = -jax.nn.softplus(-z) - 0.5
        q_out[2 + d] = jnp.exp(-jnp.exp(log_w))
        a = jax.nn.sigmoid(a0_ref[d:d + 1, :] + _mm(a_lo, aup_ref[d]))
        k_mod = k * (1.0 + (a - 1.0) * ka_ref[...])
        q_out[4 + d] = k_mod
        q_out[6 + d] = kap * a
        bonus = bonus + seg(r * k_mod * rk_ref[...]) * v
    bonus_out[...] = bonus


def _rwkv_prep(p, first, last, conv_w, w0, w_up, a0, a_up, g_up, k_k, k_a, r_k, tm):
    n = p.shape[0]
    rw = k_k.shape[0]
    lw, la, lg = w_up.shape[1], a_up.shape[1], g_up.shape[0]
    tail_w = lw + la + lg
    tail_blk = (p.shape[1] - tail_w) // tail_w
    nb8 = n // 8
    row2 = lambda i, f, l: (0, 0)
    row3 = lambda i, f, l: (0, 0, 0)
    one = jax.ShapeDtypeStruct((n, rw), F32)
    one_spec = pl.BlockSpec((tm, rw), lambda i, f, l: (i, 0))
    return pl.pallas_call(
        functools.partial(_rwkv_prep_kernel, rw=rw, lw=lw, la=la),
        grid_spec=pltpu.PrefetchScalarGridSpec(
            num_scalar_prefetch=2,
            grid=(n // tm,),
            in_specs=[
                pl.BlockSpec((tm, 3 * rw), lambda i, f, l: (i, 0)),
                pl.BlockSpec((8, 3 * rw), lambda i, f, l: (jnp.maximum(i * (tm // 8) - 1, 0), 0)),
                pl.BlockSpec((8, 3 * rw), lambda i, f, l: (jnp.minimum((i + 1) * (tm // 8), nb8 - 1), 0)),
                pl.BlockSpec((tm, tail_w), lambda i, f, l: (i, tail_blk)),
                pl.BlockSpec((3, 3 * rw), row2),
                pl.BlockSpec((2, rw), row2),
                pl.BlockSpec((2, lw, rw), row3),
                pl.BlockSpec((2, rw), row2),
                pl.BlockSpec((2, la, rw), row3),
                pl.BlockSpec((lg, rw), row2),
                pl.BlockSpec((1, rw), row2),
                pl.BlockSpec((1, rw), row2),
                pl.BlockSpec((1, rw), row2),
            ],
            out_specs=[pl.BlockSpec((8, tm, rw), lambda i, f, l: (0, i, 0)), one_spec, one_spec, one_spec],
        ),
        out_shape=[jax.ShapeDtypeStruct((8, n, rw), F32), one, one, one],
        compiler_params=_cparams(("arbitrary",)),
        name="rwkv_prep",
    )(first, last, p, p, p, p, conv_w, w0, w_up, a0, a_up, g_up,
      k_k.reshape(1, rw), k_a.reshape(1, rw), r_k.reshape(1, rw))


def _to_streams_kernel(x_ref, o_ref, xt_ref, *, split_value):
    batch, tm, rw = x_ref.shape
    heads = rw // HEAD_DIM
    for b in range(batch):
        xt_ref[b] = x_ref[b].T
    rows = HEAD_DIM // 2 if split_value else HEAD_DIM
    for j in range(rows):
        second = j + HEAD_DIM // 2 if split_value else j
        parts = [xt_ref[b, pl.ds(j, heads, stride=HEAD_DIM), :] for b in range(batch)]
        parts += [xt_ref[b, pl.ds(second, heads, stride=HEAD_DIM), :] for b in range(batch)]
        slab = jnp.concatenate(parts, axis=0).T
        if split_value:
            o_ref[:, j, :] = slab
        else:
            o_ref[j] = slab


def _to_streams(q, v, batch, tm):
    nq, n, rw = q.shape
    seq = n // batch
    scratch = [pltpu.VMEM((batch, rw, tm), F32)]
    qs = pl.pallas_call(
        functools.partial(_to_streams_kernel, split_value=False),
        grid=(nq, seq // tm),
        in_specs=[pl.BlockSpec((None, batch, tm, rw), lambda a, i: (a, 0, i, 0))],
        out_specs=pl.BlockSpec((None, HEAD_DIM, tm, LANES), lambda a, i: (a, 0, i, 0)),
        out_shape=jax.ShapeDtypeStruct((nq, HEAD_DIM, seq, LANES), F32),
        scratch_shapes=scratch,
        compiler_params=_cparams(("arbitrary", "arbitrary")),
        name="to_streams",
    )(q.reshape(nq, batch, seq, rw))
    vs = pl.pallas_call(
        functools.partial(_to_streams_kernel, split_value=True),
        grid=(seq // tm,),
        in_specs=[pl.BlockSpec((batch, tm, rw), lambda i: (0, i, 0))],
        out_specs=pl.BlockSpec((tm, HEAD_DIM // 2, LANES), lambda i: (i, 0, 0)),
        out_shape=jax.ShapeDtypeStruct((seq, HEAD_DIM // 2, LANES), F32),
        scratch_shapes=scratch,
        compiler_params=_cparams(("arbitrary",)),
        name="to_streams_v",
    )(v.reshape(batch, seq, rw))
    return qs, vs


def _rwkv_scan_kernel(rf, kapf, wf, kf, bf, vf, rb, kapb, wb, kb, bb, vb, yf, yb, s_ref):
    hd, tc, ns = rf.shape
    half = vf.shape[1]

    @pl.when(pl.program_id(0) == 0)
    def _():
        s_ref[...] = jnp.zeros_like(s_ref)

    def one_step(d, t, r_ref, kap_ref, w_ref, k_ref, b_ref, v_ref, y_ref):
        def row(ref, kk):
            return ref[kk, pl.ds(t, 1), :]

        sa = [jnp.zeros((half, ns), F32), jnp.zeros((half, ns), F32)]
        for kk in range(hd):
            sa[kk % 2] = sa[kk % 2] + s_ref[d, kk] * row(kap_ref, kk)
        sa = sa[0] + sa[1]
        vv = v_ref[t]
        y = [jnp.zeros((half, ns), F32), jnp.zeros((half, ns), F32)]
        for kk in range(hd):
            s_new = s_ref[d, kk] * row(w_ref, kk) + (vv * row(k_ref, kk) - sa * row(b_ref, kk))
            s_ref[d, kk] = s_new
            y[kk % 2] = y[kk % 2] + s_new * row(r_ref, kk)
        y_ref[t] = y[0] + y[1]

    def step(j, carry):
        one_step(0, j, rf, kapf, wf, kf, bf, vf, yf)
        one_step(1, tc - 1 - j, rb, kapb, wb, kb, bb, vb, yb)
        return carry

    lax.fori_loop(0, tc, step, 0)


def _rwkv_scan(qs, vs, t_ctx, tc):
    _, hd, seq, ns = qs.shape
    half = vs.shape[1]
    nb = seq // tc
    nbc = t_ctx // tc

    def back_block(i):
        return jnp.where(i < nbc, nbc - 1 - i, nb - 1 - (i - nbc))

    def qspec(a, back):
        if back:
            return pl.BlockSpec((None, hd, tc, ns), lambda i: (a, 0, back_block(i), 0))
        return pl.BlockSpec((None, hd, tc, ns), lambda i: (a, 0, i, 0))

    def vspec(back):
        if back:
            return pl.BlockSpec((tc, half, ns), lambda i: (back_block(i), 0, 0))
        return pl.BlockSpec((tc, half, ns), lambda i: (i, 0, 0))

    in_specs = ([qspec(a, False) for a in (0, 1, 2, 4, 6)] + [vspec(False)]
                + [qspec(a, True) for a in (0, 1, 3, 5, 7)] + [vspec(True)])
    out = jax.ShapeDtypeStruct((seq, half, ns), F32)
    return pl.pallas_call(
        _rwkv_scan_kernel,
        grid=(nb,),
        in_specs=in_specs,
        out_specs=[vspec(False), vspec(True)],
        out_shape=[out, out],
        scratch_shapes=[pltpu.VMEM((2, hd, half, ns), F32)],
        compiler_params=_cparams(("arbitrary",)),
        name="rwkv_scan",
    )(*([qs] * 5 + [vs] + [qs] * 5 + [vs]))


def _rwkv_finish_kernel(yf_ref, yb_ref, bonus_ref, gate_ref, gain_ref, bias_ref, o_ref, xt_ref):
    batch, tm, rw = bonus_ref.shape
    heads = rw // HEAD_DIM
    half = yf_ref.shape[1]
    for j in range(half):
        slab = (yf_ref[:, j, :] + yb_ref[:, j, :]).T
        for g in range(2 * batch):
            xt_ref[g % batch, pl.ds(j + half * (g // batch), heads, stride=HEAD_DIM), :] = (
                slab[g * heads:(g + 1) * heads])
    ones_bd = _segment_ones()
    for b in range(batch):
        y = xt_ref[b].T
        outs = []
        for c in range(rw // LANES):
            yc = y[:, c * LANES:(c + 1) * LANES]
            mu = _segsum(yc, ones_bd) * (1.0 / HEAD_DIM)
            dev = yc - mu
            var = _segsum(dev * dev, ones_bd) * (1.0 / HEAD_DIM)
            outs.append(dev * lax.rsqrt(var + GN_EPS))
        yn = jnp.concatenate(outs, axis=1)
        o_ref[b] = ((yn * gain_ref[...] + bias_ref[...] + bonus_ref[b]) * gate_ref[b]).astype(o_ref.dtype)


def _rwkv_finish(yf, yb, bonus, gate, gain, bias, batch, tm):
    seq, half, ns = yf.shape
    n, rw = bonus.shape
    yspec = pl.BlockSpec((tm, half, ns), lambda i: (i, 0, 0))
    spec = pl.BlockSpec((batch, tm, rw), lambda i: (0, i, 0))
    vec = pl.BlockSpec((1, rw), lambda i: (0, 0))
    out = pl.pallas_call(
        _rwkv_finish_kernel,
        grid=(seq // tm,),
        in_specs=[yspec, yspec, spec, spec, vec, vec],
        out_specs=spec,
        out_shape=jax.ShapeDtypeStruct((batch, seq, rw), BF16),
        scratch_shapes=[pltpu.VMEM((batch, rw, tm), F32)],
        compiler_params=_cparams(("arbitrary",)),
        name="rwkv_finish",
    )(yf, yb, bonus.reshape(batch, seq, rw), gate.reshape(batch, seq, rw),
      gain.reshape(1, rw), bias.reshape(1, rw))
    return out.reshape(n, rw)


def _da_prep_kernel(x_ref, cos_ref, sin_ref, qg_ref, kg_ref, o_ref):
    s = pl.program_id(1)

    @pl.when(s == 2)
    def _():
        o_ref[...] = x_ref[...].astype(BF16)

    @pl.when(s < 2)
    def _():
        ones_bd = _segment_ones()
        gain = jnp.where(s == 0, qg_ref[...] * (HEAD_DIM ** -0.5), kg_ref[...])
        cos = cos_ref[...]
        sin = sin_ref[...]
        lane = lax.broadcasted_iota(jnp.int32, cos.shape, 1)
        even = ((lane >> 4) & 1) == 0
        x = x_ref[...]
        for c in range(x.shape[1] // LANES):
            xc = x[:, c * LANES:(c + 1) * LANES]
            ms = _segsum(xc * xc, ones_bd) * (1.0 / HEAD_DIM)
            xn = xc * lax.rsqrt(ms + NORM_EPS) * gain
            swapped = jnp.where(even, pltpu.roll(xn, LANES - HEAD_DIM // 4, axis=1),
                                pltpu.roll(xn, HEAD_DIM // 4, axis=1))
            o_ref[:, c * LANES:(c + 1) * LANES] = (xn * cos + swapped * sin).astype(BF16)


def _da_prep(p, cos_t, sin_t, q_gain, k_gain, col0, width, tm):
    n = p.shape[0]
    blk0 = col0 // width
    qg = jnp.tile(q_gain, LANES // HEAD_DIM).reshape(1, LANES)
    kg = jnp.tile(k_gain, LANES // HEAD_DIM).reshape(1, LANES)
    return pl.pallas_call(
        _da_prep_kernel,
        grid=(n // tm, 3),
        in_specs=[
            pl.BlockSpec((tm, width), lambda i, s: (i, blk0 + s)),
            pl.BlockSpec((tm, LANES), lambda i, s: (i, 0)),
            pl.BlockSpec((tm, LANES), lambda i, s: (i, 0)),
            pl.BlockSpec((1, LANES), lambda i, s: (0, 0)),
            pl.BlockSpec((1, LANES), lambda i, s: (0, 0)),
        ],
        out_specs=pl.BlockSpec((None, tm, width), lambda i, s: (s, i, 0)),
        out_shape=jax.ShapeDtypeStruct((3, n, width), BF16),
        compiler_params=_cparams(("arbitrary", "arbitrary")),
        name="da_prep",
    )(p, cos_t, sin_t, qg, kg)


def _diff_attn_kernel(q_ref, k_ref, v_ref, lam_ref, sg_ref, o_ref, *, lam_init, t_ctx):
    tq = q_ref.shape[0]
    lv = lam_ref[...]
    lam = (jnp.exp(jnp.sum(lv[0:1] * lv[1:2], axis=1, keepdims=True))
           - jnp.exp(jnp.sum(lv[2:3] * lv[3:4], axis=1, keepdims=True)) + lam_init)

    def attend(keys, vals):
        q = q_ref[...]
        lane = lax.broadcasted_iota(jnp.int32, q.shape, 1)
        zero = jnp.zeros_like(q)
        outs = []
        for m in range(2):
            qm = jnp.where((lane >> 6) == m, q, zero)
            s = lax.dot_general(qm, keys, (((1,), (1,)), ((), ())), preferred_element_type=F32)
            p = jnp.exp(s - jnp.max(s, axis=1, keepdims=True))
            den = jnp.sum(p, axis=1, keepdims=True)
            outs.append(jnp.dot(p.astype(BF16), vals, preferred_element_type=F32) / den)
        o = outs[0] - lam * outs[1]
        ms = jnp.mean(o * o, axis=1, keepdims=True)
        o_ref[...] = (o * lax.rsqrt(ms + NORM_EPS) * sg_ref[...] * (1.0 - lam_init)).astype(o_ref.dtype)

    is_ctx = pl.program_id(2) < t_ctx // tq

    @pl.when(is_ctx)
    def _():
        attend(k_ref[0:t_ctx, :], v_ref[0:t_ctx, :])

    @pl.when(jnp.logical_not(is_ctx))
    def _():
        attend(k_ref[...], v_ref[...])


def _diff_attention(qkv, lam_vec, sub_gain, lam_init, batch, t_ctx, tq):
    _, n, width = qkv.shape
    heads = width // LANES
    seq = n // batch
    nq = seq // tq
    return pl.pallas_call(
        functools.partial(_diff_attn_kernel, lam_init=lam_init, t_ctx=t_ctx),
        grid=(batch, heads, nq),
        in_specs=[
            pl.BlockSpec((None, tq, LANES), lambda b, h, i: (0, b * nq + i, h)),
            pl.BlockSpec((None, seq, LANES), lambda b, h, i: (1, b, h)),
            pl.BlockSpec((None, seq, LANES), lambda b, h, i: (2, b, h)),
            pl.BlockSpec((4, HEAD_DIM), lambda b, h, i: (0, 0)),
            pl.BlockSpec((1, LANES), lambda b, h, i: (0, 0)),
        ],
        out_specs=pl.BlockSpec((tq, LANES), lambda b, h, i: (b * nq + i, h)),
        out_shape=jax.ShapeDtypeStruct((n, width), BF16),
        compiler_params=_cparams(("arbitrary", "arbitrary", "arbitrary")),
        name="diff_attn",
    )(qkv, qkv, qkv, lam_vec, sub_gain.reshape(1, LANES))


def _peer_scores_kernel(q_ref, keys_ref, o_ref):
    o_ref[...] = lax.dot_general(keys_ref[...], q_ref[...], (((1,), (1,)), ((), ())),
                                 precision=lax.Precision.HIGHEST, preferred_element_type=F32)


def _peer_scores(q, keys, tt):
    n = q.shape[0]
    hp, nk, half = keys.shape
    return pl.pallas_call(
        _peer_scores_kernel,
        grid=(n // tt, hp),
        in_specs=[
            pl.BlockSpec((tt, half), lambda i, j: (i, j)),
            pl.BlockSpec((None, nk, half), lambda i, j: (j, 0, 0)),
        ],
        out_specs=pl.BlockSpec((None, nk, tt), lambda i, j: (j, 0, i)),
        out_shape=jax.ShapeDtypeStruct((hp, nk, n), F32),
        compiler_params=_cparams(("arbitrary", "arbitrary")),
        name="peer_scores",
    )(q, keys)


SPLIT_B = 4


def _peer_select_kernel(s_ref, e1_ref, bi_ref, e2_ref, r2_ref):
    nk, tl = s_ref.shape[1:]
    k = PEER_TOPK
    neg = -jnp.inf
    unranked = float(2 * k)

    def top(s):
        key = lax.broadcasted_iota(jnp.int32, s.shape, 0)
        work = s
        rank = jnp.full(s.shape, unranked, F32)
        vals = []
        for r in range(k):
            m = jnp.max(work, axis=0, keepdims=True)
            idx = jnp.min(jnp.where(work == m, key, nk), axis=0, keepdims=True)
            hit = key == idx
            work = jnp.where(hit, neg, work)
            rank = jnp.where(hit, float(r), rank)
            vals.append(m)
        return jnp.concatenate(vals, axis=0), rank

    s1 = s_ref[0]
    s2 = s_ref[1]
    v1, rank1 = top(s1)
    v2, rank2 = top(s2)
    sub = lax.broadcasted_iota(jnp.int32, (k, tl), 0)
    work, flat = [], []
    for b in range(SPLIT_B):
        ok = sub <= k // (b + 1) - 1
        work.append(jnp.where(ok, v1 + v2[b:b + 1], neg))
        flat.append(jnp.where(ok, sub * k + b, -1))
    n_a = k // (SPLIT_B + 1)
    for a in range(n_a):
        ok = (sub >= SPLIT_B) & (sub <= k // (a + 1) - 1)
        work.append(jnp.where(ok, v1[a:a + 1] + v2, neg))
        flat.append(jnp.where(ok, sub + a * k, -1))
    count = [jnp.zeros((k, tl), F32) for _ in work]
    top_sum = v1[0:1] + v2[0:1]
    z = jnp.zeros((1, tl), F32)
    for _ in range(k):
        m = jnp.max(functools.reduce(jnp.maximum, work), axis=0, keepdims=True)
        cand = functools.reduce(jnp.minimum, [jnp.where(w == m, f, k * k) for w, f in zip(work, flat)])
        idx = jnp.min(cand, axis=0, keepdims=True)
        z = z + jnp.exp(m - top_sum)
        for n in range(len(work)):
            hit = flat[n] == idx
            work[n] = jnp.where(hit, neg, work[n])
            count[n] = count[n] + jnp.where(hit, 1.0, 0.0)
    per_a = functools.reduce(jnp.add, count[:SPLIT_B])
    for a in range(n_a):
        extra = jnp.sum(count[SPLIT_B + a], axis=0, keepdims=True)
        per_a = per_a + jnp.where(sub == a, extra, 0.0)
    bi = jnp.zeros(s1.shape, F32)
    for a in range(k):
        bi = jnp.where(rank1 == float(a), per_a[a:a + 1], bi)
    bi_ref[...] = bi
    e1_ref[...] = jnp.exp(s1 - v1[0:1]) / z
    e2_ref[...] = jnp.exp(s2 - v2[0:1])
    r2_ref[...] = rank2


def _peer_select(scores, heads, tl):
    hp, nk, n = scores.shape
    out = jax.ShapeDtypeStruct((heads, nk, n), F32)
    spec = pl.BlockSpec((None, nk, tl), lambda t, h: (h, 0, t))
    return pl.pallas_call(
        _peer_select_kernel,
        grid=(n // tl, heads),
        in_specs=[pl.BlockSpec((None, 2, nk, tl), lambda t, h: (h, 0, 0, t))],
        out_specs=[spec] * 4,
        out_shape=[out] * 4,
        compiler_params=_cparams(("arbitrary", "arbitrary")),
        name="peer_select",
    )(scores.reshape(heads, 2, nk, n))


def _peer_dense_kernel(h_ref, u_ref, v_ref, e1_ref, bi_ref, e2_ref, r2_ref, o_ref, *, nk, chunk):
    e = pl.program_id(1)
    heads = e1_ref.shape[0]
    et = u_ref.shape[0]

    @pl.when(e == 0)
    def _():
        o_ref[...] = jnp.zeros_like(o_ref)

    total = None
    for c in range(et // chunk):
        a = lax.dot_general(u_ref[c * chunk:(c + 1) * chunk, :], h_ref[...], (((1,), (1,)), ((), ())),
                            preferred_element_type=F32)
        act = 0.5 * a * (1.0 + lax.erf(a * (2.0 ** -0.5)))
        parts = []
        for ii in range(chunk // nk):
            i = (e * et + c * chunk) // nk + ii
            g = jnp.zeros((nk, a.shape[1]), F32)
            for hh in range(heads):
                e1 = e1_ref[hh, pl.ds(i, 1), :]
                bi = bi_ref[hh, pl.ds(i, 1), :]
                g = g + jnp.where(r2_ref[hh] < bi, e2_ref[hh] * e1, 0.0)
            parts.append(g * act[ii * nk:(ii + 1) * nk])
        m = jnp.concatenate(parts, axis=0) if len(parts) > 1 else parts[0]
        contrib = jnp.dot(m.T.astype(BF16), v_ref[c * chunk:(c + 1) * chunk, :], preferred_element_type=F32)
        total = contrib if total is None else total + contrib
    o_ref[...] += total


def _peer_dense(h, u, v, e1, bi, e2, r2, tl, et, chunk):
    n, d = h.shape
    ne = u.shape[0]
    heads, nk, _ = e1.shape
    fac = pl.BlockSpec((heads, nk, tl), lambda t, e: (0, 0, t))
    return pl.pallas_call(
        functools.partial(_peer_dense_kernel, nk=nk, chunk=chunk),
        grid=(n // tl, ne // et),
        in_specs=[
            pl.BlockSpec((tl, d), lambda t, e: (t, 0)),
            pl.BlockSpec((et, d), lambda t, e: (e, 0)),
            pl.BlockSpec((et, d), lambda t, e: (e, 0)),
            fac, fac, fac, fac,
        ],
        out_specs=pl.BlockSpec((tl, d), lambda t, e: (t, 0)),
        out_shape=jax.ShapeDtypeStruct((n, d), F32),
        compiler_params=_cparams(("arbitrary", "arbitrary")),
        name="peer_dense",
    )(h, u, v, e1, bi, e2, r2)


def _gated_add_kernel(gid_ref, x_ref, y_ref, gate_ref, o_ref):
    del gid_ref
    o_ref[...] = x_ref[...] + gate_ref[...] * y_ref[...]


def _gated_add(x, y, gid, gate, tm):
    n, d = x.shape
    spec = pl.BlockSpec((tm, d), lambda i, g: (i, 0))
    return pl.pallas_call(
        _gated_add_kernel,
        grid_spec=pltpu.PrefetchScalarGridSpec(
            num_scalar_prefetch=1,
            grid=(n // tm,),
            in_specs=[spec, spec, pl.BlockSpec((None, 1, d), lambda i, g: (g[i], 0, 0))],
            out_specs=spec,
        ),
        out_shape=jax.ShapeDtypeStruct((n, d), F32),
        compiler_params=_cparams(("arbitrary",)),
        name="gated_add",
    )(gid, x, y, gate)


def _rope_tables(t_lat, t_ctx, batch):
    rows = t_lat // GRID_W
    row = jnp.repeat(jnp.arange(rows, dtype=jnp.int32), GRID_W)
    col = jnp.tile(jnp.arange(GRID_W, dtype=jnp.int32), rows)
    nfreq = HEAD_DIM // 4
    inv_freq = ROPE_THETA ** (-jnp.arange(nfreq, dtype=F32) / nfreq)
    ang_r = row.astype(F32)[:, None] * inv_freq
    ang_c = col.astype(F32)[:, None] * inv_freq
    cos = jnp.concatenate([jnp.cos(ang_r)] * 2 + [jnp.cos(ang_c)] * 2, axis=1)
    sin = jnp.concatenate([-jnp.sin(ang_r), jnp.sin(ang_r), -jnp.sin(ang_c), jnp.sin(ang_c)], axis=1)
    reps = LANES // HEAD_DIM
    cos = jnp.concatenate([jnp.ones((t_ctx, LANES), F32), jnp.tile(cos, (1, reps))], axis=0)
    sin = jnp.concatenate([jnp.zeros((t_ctx, LANES), F32), jnp.tile(sin, (1, reps))], axis=0)
    return jnp.tile(cos, (batch, 1)), jnp.tile(sin, (batch, 1))


def kernel(x, c, ctx, c_ctx, ada_w, ada_b, norm_gain, w_in, conv_w, rw_w0, rw_w_up, rw_a0, rw_a_up,
           rw_g_up, rw_k_k, rw_k_a, rw_r_k, rw_gn_gain, rw_gn_bias, da_q_gain, da_k_gain, da_lambda,
           da_sub_gain, w_out, peer_wq, peer_keys, peer_u, peer_v):
    batch, t_lat, d = x.shape
    t_ctx = ctx.shape[1]
    seq = t_ctx + t_lat
    n = batch * seq
    depth = ada_w.shape[0]
    rw = rw_k_k.shape[1]
    lora_w = rw_w_up.shape[2] + rw_a_up.shape[2] + rw_g_up.shape[1]
    da_w = w_out.shape[1] - rw
    p_heads, _, nk, half = peer_keys.shape[1:]
    assert 2 * batch * (rw // HEAD_DIM) == LANES, "the recurrence kernel fills the lanes with (half, batch, head)"

    tm = _tile(math.gcd(t_lat, t_ctx), 256)
    tile_pos = (jnp.arange(n // tm, dtype=jnp.int32) * tm) % seq
    gid = jnp.where(tile_pos < t_ctx, batch, jnp.arange(n // tm, dtype=jnp.int32) * tm // seq)
    first = ((tile_pos == 0) | (tile_pos == t_ctx)).astype(jnp.int32)
    last = ((tile_pos + tm == t_ctx) | (tile_pos + tm == seq)).astype(jnp.int32)

    xs = jnp.concatenate([ctx, x], axis=1).reshape(n, d)
    c8 = jnp.concatenate([c, c_ctx[None], jnp.zeros((8 - batch - 1, d), F32)], axis=0)
    mods = _modulation(c8, ada_w, ada_b)[:, :batch + 1].reshape(depth, batch + 1, 6, 1, d)
    cos_t, sin_t = _rope_tables(t_lat, t_ctx, batch)

    cols = jnp.concatenate([jnp.arange(3 * rw), jnp.arange(3 * rw + lora_w, w_in.shape[2]),
                            jnp.arange(3 * rw, 3 * rw + lora_w)])
    tn_in = _tile(w_in.shape[2], 1280, LANES)
    ts = _tile(tm, 128)

    for l in range(depth):
        lam_init = 0.8 - 0.6 * math.exp(-0.3 * l)
        mod = [mods[l, :, j] for j in range(6)]

        p = _norm_proj(xs, gid, norm_gain[l, 0], mod[0], mod[1], w_in[l][:, cols].astype(BF16), tm, tn_in)
        q8, v, bonus, gate = _rwkv_prep(
            p, first, last, conv_w[l], rw_w0[l], rw_w_up[l], rw_a0[l], rw_a_up[l], rw_g_up[l],
            rw_k_k[l], rw_k_a[l], rw_r_k[l].reshape(-1), tm)
        qs, vs = _to_streams(q8, v, batch, ts)
        yf, yb = _rwkv_scan(qs, vs, t_ctx, tc=_tile(math.gcd(t_lat, t_ctx), 32))
        rw_out = _rwkv_finish(yf, yb, bonus, gate, rw_gn_gain[l], rw_gn_bias[l], batch, ts)

        qkv = _da_prep(p, cos_t, sin_t, da_q_gain[l], da_k_gain[l], 3 * rw, da_w, tm)
        da_out = _diff_attention(qkv, da_lambda[l], da_sub_gain[l], lam_init, batch, t_ctx, tm)
        xs = _gated_proj(rw_out, da_out, w_out[l].astype(BF16), xs, gid, mod[2], tm)

        q, h = _norm_proj(xs, gid, norm_gain[l, 1], mod[3], mod[4], peer_wq[l].astype(BF16), tm,
                          _tile(peer_wq.shape[2], 2048, LANES), emit_h=True)
        scores = _peer_scores(q, peer_keys[l].reshape(2 * p_heads, nk, half), _tile(n, 1024, LANES))
        e1, bi, e2, r2 = _peer_select(scores, p_heads, _tile(n, 512, LANES))
        out = _peer_dense(h, peer_u[l].astype(BF16), peer_v[l].astype(BF16), e1, bi, e2, r2,
                          tl=_tile(n, 512, LANES), et=4 * nk, chunk=2 * nk)
        xs = _gated_add(xs, out, gid, mod[5], tm)

    return xs.reshape(batch, seq, d)[:, t_ctx:]
```

```python
import functools
import math

import jax
import jax.numpy as jnp
from jax import lax
from jax.experimental import pallas as pl
from jax.experimental.pallas import tpu as pltpu

F32 = jnp.float32
BF16 = jnp.bfloat16

HEAD_DIM = 64
LANES = 128
GRID_W = 64
ROPE_THETA = 10000.0
NORM_EPS = 1e-6
GN_EPS = 64e-5
L2_EPS = 1e-12
PEER_TOPK = 16
VMEM_LIMIT = 56 * 1024 * 1024


def _cparams(sem):
    return pltpu.CompilerParams(dimension_semantics=sem, vmem_limit_bytes=VMEM_LIMIT)


def _mm(a, b):
    return jnp.dot(a.astype(BF16), b.astype(BF16), preferred_element_type=F32)


def _segment_ones():
    r = lax.broadcasted_iota(jnp.int32, (LANES, LANES), 0) >> 6
    c = lax.broadcasted_iota(jnp.int32, (LANES, LANES), 1) >> 6
    return jnp.where(r == c, 1.0, 0.0).astype(BF16)


def _segsum(x, ones_bd):
    hi = x.astype(BF16)
    lo = (x - hi.astype(F32)).astype(BF16)
    return (jnp.dot(hi, ones_bd, preferred_element_type=F32)
            + jnp.dot(lo, ones_bd, preferred_element_type=F32))


def _tile(n, pref, align=8):
    for t in range(min(n, pref) // align * align, 0, -align):
        if n % t == 0:
            return t
    return n


def _mod_kernel(c_ref, w_ref, b_ref, o_ref):
    cv = c_ref[...]
    act = cv * jax.nn.sigmoid(cv)
    o_ref[...] = _mm(act, w_ref[...]) + b_ref[...]


def _modulation(c8, ada_w, ada_b):
    depth, d, n6 = ada_w.shape
    tn = _tile(n6, 1536)
    return pl.pallas_call(
        _mod_kernel,
        grid=(depth, n6 // tn),
        in_specs=[
            pl.BlockSpec((8, d), lambda l, j: (0, 0)),
            pl.BlockSpec((None, d, tn), lambda l, j: (l, 0, j)),
            pl.BlockSpec((None, 1, tn), lambda l, j: (l, 0, j)),
        ],
        out_specs=pl.BlockSpec((None, 8, tn), lambda l, j: (l, 0, j)),
        out_shape=jax.ShapeDtypeStruct((depth, 8, n6), F32),
        compiler_params=_cparams(("arbitrary", "arbitrary")),
        name="modulation",
    )(c8, ada_w, ada_b.reshape(depth, 1, n6))


def _norm_proj_kernel(gid_ref, x_ref, gain_ref, shift_ref, scale_ref, w_ref, o_ref, *rest, emit_h):
    del gid_ref
    hs_ref = rest[-1]

    @pl.when(pl.program_id(1) == 0)
    def _():
        x = x_ref[...]
        ms = jnp.mean(x * x, axis=-1, keepdims=True)
        y = x * lax.rsqrt(ms + NORM_EPS) * gain_ref[...]
        h = y * (1.0 + scale_ref[...]) + shift_ref[...]
        hs_ref[...] = h.astype(BF16)
        if emit_h:
            rest[0][...] = h.astype(BF16)

    o_ref[...] = jnp.dot(hs_ref[...], w_ref[...], preferred_element_type=F32)


def _norm_proj(x, gid, gain, shift, scale, w, tm, tn, emit_h=False):
    n, d = x.shape
    nout = w.shape[1]
    out_shape = [jax.ShapeDtypeStruct((n, nout), F32)]
    out_specs = [pl.BlockSpec((tm, tn), lambda i, j, g: (i, j))]
    if emit_h:
        out_shape.append(jax.ShapeDtypeStruct((n, d), BF16))
        out_specs.append(pl.BlockSpec((tm, d), lambda i, j, g: (i, 0)))
    res = pl.pallas_call(
        functools.partial(_norm_proj_kernel, emit_h=emit_h),
        grid_spec=pltpu.PrefetchScalarGridSpec(
            num_scalar_prefetch=1,
            grid=(n // tm, nout // tn),
            in_specs=[
                pl.BlockSpec((tm, d), lambda i, j, g: (i, 0)),
                pl.BlockSpec((1, d), lambda i, j, g: (0, 0)),
                pl.BlockSpec((None, 1, d), lambda i, j, g: (g[i], 0, 0)),
                pl.BlockSpec((None, 1, d), lambda i, j, g: (g[i], 0, 0)),
                pl.BlockSpec((d, tn), lambda i, j, g: (0, j)),
            ],
            out_specs=out_specs,
            scratch_shapes=[pltpu.VMEM((tm, d), BF16)],
        ),
        out_shape=out_shape,
        compiler_params=_cparams(("arbitrary", "arbitrary")),
        name="norm_proj",
    )(gid, x, gain.reshape(1, d), shift, scale, w)
    return res if emit_h else res[0]


def _norm_mod_kernel(gid_ref, x_ref, gain_ref, shift_ref, scale_ref, o_ref):
    del gid_ref
    x = x_ref[...]
    ms = jnp.mean(x * x, axis=-1, keepdims=True)
    y = x * lax.rsqrt(ms + NORM_EPS) * gain_ref[...]
    o_ref[...] = (y * (1.0 + scale_ref[...]) + shift_ref[...]).astype(BF16)


def _matmul_kernel(h_ref, w_ref, o_ref):
    o_ref[...] = jnp.dot(h_ref[...], w_ref[...], preferred_element_type=F32)


def _norm_proj_wide(x, gid, gain, shift, scale, w, tm, tm_mm, tn):
    n, d = x.shape
    nout = w.shape[1]
    h = pl.pallas_call(
        _norm_mod_kernel,
        grid_spec=pltpu.PrefetchScalarGridSpec(
            num_scalar_prefetch=1,
            grid=(n // tm,),
            in_specs=[
                pl.BlockSpec((tm, d), lambda i, g: (i, 0)),
                pl.BlockSpec((1, d), lambda i, g: (0, 0)),
                pl.BlockSpec((None, 1, d), lambda i, g: (g[i], 0, 0)),
                pl.BlockSpec((None, 1, d), lambda i, g: (g[i], 0, 0)),
            ],
            out_specs=pl.BlockSpec((tm, d), lambda i, g: (i, 0)),
        ),
        out_shape=jax.ShapeDtypeStruct((n, d), BF16),
        compiler_params=_cparams(("arbitrary",)),
        name="norm_mod",
    )(gid, x, gain.reshape(1, d), shift, scale)
    return pl.pallas_call(
        _matmul_kernel,
        grid=(nout // tn, n // tm_mm),
        in_specs=[
            pl.BlockSpec((tm_mm, d), lambda j, i: (i, 0)),
            pl.BlockSpec((d, tn), lambda j, i: (0, j)),
        ],
        out_specs=pl.BlockSpec((tm_mm, tn), lambda j, i: (i, j)),
        out_shape=jax.ShapeDtypeStruct((n, nout), F32),
        compiler_params=_cparams(("arbitrary", "arbitrary")),
        name="proj_matmul",
    )(h, w)


def _gated_proj_kernel(gid_ref, y1_ref, y2_ref, w1_ref, w2_ref, x_ref, gate_ref, o_ref):
    del gid_ref
    mix = (jnp.dot(y1_ref[...], w1_ref[...], preferred_element_type=F32)
           + jnp.dot(y2_ref[...], w2_ref[...], preferred_element_type=F32))
    o_ref[...] = x_ref[...] + gate_ref[...] * mix


def _gated_proj(y1, y2, w, x, gid, gate, tm):
    n, k1 = y1.shape
    k2 = y2.shape[1]
    d = w.shape[1]
    return pl.pallas_call(
        _gated_proj_kernel,
        grid_spec=pltpu.PrefetchScalarGridSpec(
            num_scalar_prefetch=1,
            grid=(n // tm,),
            in_specs=[
                pl.BlockSpec((tm, k1), lambda i, g: (i, 0)),
                pl.BlockSpec((tm, k2), lambda i, g: (i, 0)),
                pl.BlockSpec((k1, d), lambda i, g: (0, 0)),
                pl.BlockSpec((k2, d), lambda i, g: (k1 // k2, 0)),
                pl.BlockSpec((tm, d), lambda i, g: (i, 0)),
                pl.BlockSpec((None, 1, d), lambda i, g: (g[i], 0, 0)),
            ],
            out_specs=pl.BlockSpec((tm, d), lambda i, g: (i, 0)),
        ),
        out_shape=jax.ShapeDtypeStruct((n, d), F32),
        compiler_params=_cparams(("arbitrary",)),
        name="gated_proj",
    )(gid, y1, y2, w, w, x, gate)


def _rwkv_prep_kernel(first_ref, last_ref, x_ref, prev_ref, next_ref, tail_ref, cw_ref,
                      w0_ref, wup_ref, a0_ref, aup_ref, gup_ref, kk_ref, ka_ref, rk_ref,
                      q_out, v_out, bonus_out, gate_out, *, rw, lw, la):
    i = pl.program_id(0)
    tm = x_ref.shape[0]
    x = x_ref[...]
    row = lax.broadcasted_iota(jnp.int32, x.shape, 0)
    keep_prev = jnp.where(first_ref[i] == 1, 0.0, 1.0)
    keep_next = jnp.where(last_ref[i] == 1, 0.0, 1.0)
    halo_prev = prev_ref[7:8, :] * keep_prev
    halo_next = next_ref[0:1, :] * keep_next
    x_prev = jnp.where(row == 0, halo_prev, pltpu.roll(x, 1, axis=0))
    x_next = jnp.where(row == tm - 1, halo_next, pltpu.roll(x, tm - 1, axis=0))
    conv = x_prev * cw_ref[0:1, :] + x * cw_ref[1:2, :] + x_next * cw_ref[2:3, :]
    r = conv[:, 0:rw]
    k = conv[:, rw:2 * rw]
    v = conv[:, 2 * rw:3 * rw]
    tail = tail_ref[...]
    w_lo = jnp.tanh(tail[:, 0:lw])
    a_lo = tail[:, lw:lw + la]
    g_lo = jax.nn.sigmoid(tail[:, lw + la:])
    ones_bd = _segment_ones()

    def seg(t):
        return jnp.concatenate(
            [_segsum(t[:, c * LANES:(c + 1) * LANES], ones_bd) for c in range(rw // LANES)], axis=1)

    kap = k * kk_ref[...]
    nrm = jnp.sqrt(seg(kap * kap))
    kap = kap / jnp.maximum(nrm, L2_EPS)
    q_out[0] = r
    q_out[1] = kap
    v_out[...] = v
    gate_out[...] = _mm(g_lo, gup_ref[...])
    bonus = jnp.zeros_like(r)
    for d in range(2):
        z = w0_ref[d:d + 1, :] + _mm(w_lo, wup_ref[d])
        log_w = -jax.nn.softplus(-z) - 0.5
        q_out[2 + d] = jnp.exp(-jnp.exp(log_w))
        a = jax.nn.sigmoid(a0_ref[d:d + 1, :] + _mm(a_lo, aup_ref[d]))
        k_mod = k * (1.0 + (a - 1.0) * ka_ref[...])
        q_out[4 + d] = k_mod
        q_out[6 + d] = kap * a
        bonus = bonus + seg(r * k_mod * rk_ref[...]) * v
    bonus_out[...] = bonus


def _rwkv_prep(p, first, last, conv_w, w0, w_up, a0, a_up, g_up, k_k, k_a, r_k, tm):
    n = p.shape[0]
    rw = k_k.shape[0]
    lw, la, lg = w_up.shape[1], a_up.shape[1], g_up.shape[0]
    tail_w = lw + la + lg
    tail_blk = (p.shape[1] - tail_w) // tail_w
    nb8 = n // 8
    row2 = lambda i, f, l: (0, 0)
    row3 = lambda i, f, l: (0, 0, 0)
    one = jax.ShapeDtypeStruct((n, rw), F32)
    one_spec = pl.BlockSpec((tm, rw), lambda i, f, l: (i, 0))
    return pl.pallas_call(
        functools.partial(_rwkv_prep_kernel, rw=rw, lw=lw, la=la),
        grid_spec=pltpu.PrefetchScalarGridSpec(
            num_scalar_prefetch=2,
            grid=(n // tm,),
            in_specs=[
                pl.BlockSpec((tm, 3 * rw), lambda i, f, l: (i, 0)),
                pl.BlockSpec((8, 3 * rw), lambda i, f, l: (jnp.maximum(i * (tm // 8) - 1, 0), 0)),
                pl.BlockSpec((8, 3 * rw), lambda i, f, l: (jnp.minimum((i + 1) * (tm // 8), nb8 - 1), 0)),
                pl.BlockSpec((tm, tail_w), lambda i, f, l: (i, tail_blk)),
                pl.BlockSpec((3, 3 * rw), row2),
                pl.BlockSpec((2, rw), row2),
                pl.BlockSpec((2, lw, rw), row3),
                pl.BlockSpec((2, rw), row2),
                pl.BlockSpec((2, la, rw), row3),
                pl.BlockSpec((lg, rw), row2),
                pl.BlockSpec((1, rw), row2),
                pl.BlockSpec((1, rw), row2),
                pl.BlockSpec((1, rw), row2),
            ],
            out_specs=[pl.BlockSpec((8, tm, rw), lambda i, f, l: (0, i, 0)), one_spec, one_spec, one_spec],
        ),
        out_shape=[jax.ShapeDtypeStruct((8, n, rw), F32), one, one, one],
        compiler_params=_cparams(("arbitrary",)),
        name="rwkv_prep",
    )(first, last, p, p, p, p, conv_w, w0, w_up, a0, a_up, g_up,
      k_k.reshape(1, rw), k_a.reshape(1, rw), r_k.reshape(1, rw))


def _to_streams_kernel(x_ref, o_ref, xt_ref, *, split_value):
    batch, tm, rw = x_ref.shape
    heads = rw // HEAD_DIM
    for b in range(batch):
        xt_ref[b] = x_ref[b].T
    rows = HEAD_DIM // 2 if split_value else HEAD_DIM
    for j in range(rows):
        second = j + HEAD_DIM // 2 if split_value else j
        parts = [xt_ref[b, pl.ds(j, heads, stride=HEAD_DIM), :] for b in range(batch)]
        parts += [xt_ref[b, pl.ds(second, heads, stride=HEAD_DIM), :] for b in range(batch)]
        slab = jnp.concatenate(parts, axis=0).T
        if split_value:
            o_ref[:, j, :] = slab
        else:
            o_ref[j] = slab


def _to_streams(q, v, batch, tm):
    nq, n, rw = q.shape
    seq = n // batch
    scratch = [pltpu.VMEM((batch, rw, tm), F32)]
    qs = pl.pallas_call(
        functools.partial(_to_streams_kernel, split_value=False),
        grid=(nq, seq // tm),
        in_specs=[pl.BlockSpec((None, batch, tm, rw), lambda a, i: (a, 0, i, 0))],
        out_specs=pl.BlockSpec((None, HEAD_DIM, tm, LANES), lambda a, i: (a, 0, i, 0)),
        out_shape=jax.ShapeDtypeStruct((nq, HEAD_DIM, seq, LANES), F32),
        scratch_shapes=scratch,
        compiler_params=_cparams(("arbitrary", "arbitrary")),
        name="to_streams",
    )(q.reshape(nq, batch, seq, rw))
    vs = pl.pallas_call(
        functools.partial(_to_streams_kernel, split_value=True),
        grid=(seq // tm,),
        in_specs=[pl.BlockSpec((batch, tm, rw), lambda i: (0, i, 0))],
        out_specs=pl.BlockSpec((tm, HEAD_DIM // 2, LANES), lambda i: (i, 0, 0)),
        out_shape=jax.ShapeDtypeStruct((seq, HEAD_DIM // 2, LANES), F32),
        scratch_shapes=scratch,
        compiler_params=_cparams(("arbitrary",)),
        name="to_streams_v",
    )(v.reshape(batch, seq, rw))
    return qs, vs


def _rwkv_scan_kernel(rf, kapf, wf, kf, bf, vf, rb, kapb, wb, kb, bb, vb, yf, yb, s_ref):
    hd, tc, ns = rf.shape
    half = vf.shape[1]

    @pl.when(pl.program_id(0) == 0)
    def _():
        s_ref[...] = jnp.zeros_like(s_ref)

    def one_step(d, t, r_ref, kap_ref, w_ref, k_ref, b_ref, v_ref, y_ref):
        def row(ref, kk):
            return ref[kk, pl.ds(t, 1), :]

        sa = [jnp.zeros((half, ns), F32), jnp.zeros((half, ns), F32)]
        for kk in range(hd):
            sa[kk % 2] = sa[kk % 2] + s_ref[d, kk] * row(kap_ref, kk)
        sa = sa[0] + sa[1]
        vv = v_ref[t]
        y = [jnp.zeros((half, ns), F32), jnp.zeros((half, ns), F32)]
        for kk in range(hd):
            s_new = s_ref[d, kk] * row(w_ref, kk) + (vv * row(k_ref, kk) - sa * row(b_ref, kk))
            s_ref[d, kk] = s_new
            y[kk % 2] = y[kk % 2] + s_new * row(r_ref, kk)
        y_ref[t] = y[0] + y[1]

    def step(j, carry):
        one_step(0, j, rf, kapf, wf, kf, bf, vf, yf)
        one_step(1, tc - 1 - j, rb, kapb, wb, kb, bb, vb, yb)
        return carry

    lax.fori_loop(0, tc, step, 0)


def _rwkv_scan(qs, vs, t_ctx, tc):
    _, hd, seq, ns = qs.shape
    half = vs.shape[1]
    nb = seq // tc
    nbc = t_ctx // tc

    def back_block(i):
        return jnp.where(i < nbc, nbc - 1 - i, nb - 1 - (i - nbc))

    def qspec(a, back):
        if back:
            return pl.BlockSpec((None, hd, tc, ns), lambda i: (a, 0, back_block(i), 0))
        return pl.BlockSpec((None, hd, tc, ns), lambda i: (a, 0, i, 0))

    def vspec(back):
        if back:
            return pl.BlockSpec((tc, half, ns), lambda i: (back_block(i), 0, 0))
        return pl.BlockSpec((tc, half, ns), lambda i: (i, 0, 0))

    in_specs = ([qspec(a, False) for a in (0, 1, 2, 4, 6)] + [vspec(False)]
                + [qspec(a, True) for a in (0, 1, 3, 5, 7)] + [vspec(True)])
    out = jax.ShapeDtypeStruct((seq, half, ns), F32)
    return pl.pallas_call(
        _rwkv_scan_kernel,
        grid=(nb,),
        in_specs=in_specs,
        out_specs=[vspec(False), vspec(True)],
        out_shape=[out, out],
        scratch_shapes=[pltpu.VMEM((2, hd, half, ns), F32)],
        compiler_params=_cparams(("arbitrary",)),
        name="rwkv_scan",
    )(*([qs] * 5 + [vs] + [qs] * 5 + [vs]))


def _rwkv_finish_kernel(yf_ref, yb_ref, bonus_ref, gate_ref, gain_ref, bias_ref, o_ref, xt_ref):
    batch, tm, rw = bonus_ref.shape
    heads = rw // HEAD_DIM
    half = yf_ref.shape[1]
    for j in range(half):
        slab = (yf_ref[:, j, :] + yb_ref[:, j, :]).T
        for g in range(2 * batch):
            xt_ref[g % batch, pl.ds(j + half * (g // batch), heads, stride=HEAD_DIM), :] = (
                slab[g * heads:(g + 1) * heads])
    ones_bd = _segment_ones()
    for b in range(batch):
        y = xt_ref[b].T
        outs = []
        for c in range(rw // LANES):
            yc = y[:, c * LANES:(c + 1) * LANES]
            mu = _segsum(yc, ones_bd) * (1.0 / HEAD_DIM)
            dev = yc - mu
            var = _segsum(dev * dev, ones_bd) * (1.0 / HEAD_DIM)
            outs.append(dev * lax.rsqrt(var + GN_EPS))
        yn = jnp.concatenate(outs, axis=1)
        o_ref[b] = ((yn * gain_ref[...] + bias_ref[...] + bonus_ref[b]) * gate_ref[b]).astype(o_ref.dtype)


def _rwkv_finish(yf, yb, bonus, gate, gain, bias, batch, tm):
    seq, half, ns = yf.shape
    n, rw = bonus.shape
    yspec = pl.BlockSpec((tm, half, ns), lambda i: (i, 0, 0))
    spec = pl.BlockSpec((batch, tm, rw), lambda i: (0, i, 0))
    vec = pl.BlockSpec((1, rw), lambda i: (0, 0))
    out = pl.pallas_call(
        _rwkv_finish_kernel,
        grid=(seq // tm,),
        in_specs=[yspec, yspec, spec, spec, vec, vec],
        out_specs=spec,
        out_shape=jax.ShapeDtypeStruct((batch, seq, rw), BF16),
        scratch_shapes=[pltpu.VMEM((batch, rw, tm), F32)],
        compiler_params=_cparams(("arbitrary",)),
        name="rwkv_finish",
    )(yf, yb, bonus.reshape(batch, seq, rw), gate.reshape(batch, seq, rw),
      gain.reshape(1, rw), bias.reshape(1, rw))
    return out.reshape(n, rw)


def _da_prep_kernel(x_ref, cos_ref, sin_ref, qg_ref, kg_ref, o_ref):
    s = pl.program_id(1)

    @pl.when(s == 2)
    def _():
        o_ref[...] = x_ref[...].astype(BF16)

    @pl.when(s < 2)
    def _():
        ones_bd = _segment_ones()
        gain = jnp.where(s == 0, qg_ref[...] * (HEAD_DIM ** -0.5), kg_ref[...])
        cos = cos_ref[...]
        sin = sin_ref[...]
        lane = lax.broadcasted_iota(jnp.int32, cos.shape, 1)
        even = ((lane >> 4) & 1) == 0
        x = x_ref[...]
        for c in range(x.shape[1] // LANES):
            xc = x[:, c * LANES:(c + 1) * LANES]
            ms = _segsum(xc * xc, ones_bd) * (1.0 / HEAD_DIM)
            xn = xc * lax.rsqrt(ms + NORM_EPS) * gain
            swapped = jnp.where(even, pltpu.roll(xn, LANES - HEAD_DIM // 4, axis=1),
                                pltpu.roll(xn, HEAD_DIM // 4, axis=1))
            o_ref[:, c * LANES:(c + 1) * LANES] = (xn * cos + swapped * sin).astype(BF16)


def _da_prep(p, cos_t, sin_t, q_gain, k_gain, col0, width, tm):
    n = p.shape[0]
    blk0 = col0 // width
    qg = jnp.tile(q_gain, LANES // HEAD_DIM).reshape(1, LANES)
    kg = jnp.tile(k_gain, LANES // HEAD_DIM).reshape(1, LANES)
    return pl.pallas_call(
        _da_prep_kernel,
        grid=(n // tm, 3),
        in_specs=[
            pl.BlockSpec((tm, width), lambda i, s: (i, blk0 + s)),
            pl.BlockSpec((tm, LANES), lambda i, s: (i, 0)),
            pl.BlockSpec((tm, LANES), lambda i, s: (i, 0)),
            pl.BlockSpec((1, LANES), lambda i, s: (0, 0)),
            pl.BlockSpec((1, LANES), lambda i, s: (0, 0)),
        ],
        out_specs=pl.BlockSpec((None, tm, width), lambda i, s: (s, i, 0)),
        out_shape=jax.ShapeDtypeStruct((3, n, width), BF16),
        compiler_params=_cparams(("arbitrary", "arbitrary")),
        name="da_prep",
    )(p, cos_t, sin_t, qg, kg)


def _diff_attn_kernel(q_ref, k_ref, v_ref, lam_ref, sg_ref, o_ref, *, lam_init, t_ctx):
    tq = q_ref.shape[0]
    lv = lam_ref[...]
    lam = (jnp.exp(jnp.sum(lv[0:1] * lv[1:2], axis=1, keepdims=True))
           - jnp.exp(jnp.sum(lv[2:3] * lv[3:4], axis=1, keepdims=True)) + lam_init)

    def attend(keys, vals):
        q = q_ref[...]
        lane = lax.broadcasted_iota(jnp.int32, q.shape, 1)
        zero = jnp.zeros_like(q)
        outs = []
        for m in range(2):
            qm = jnp.where((lane >> 6) == m, q, zero)
            s = lax.dot_general(qm, keys, (((1,), (1,)), ((), ())), preferred_element_type=F32)
            p = jnp.exp(s - jnp.max(s, axis=1, keepdims=True))
            den = jnp.sum(p, axis=1, keepdims=True)
            outs.append(jnp.dot(p.astype(BF16), vals, preferred_element_type=F32) / den)
        o = outs[0] - lam * outs[1]
        ms = jnp.mean(o * o, axis=1, keepdims=True)
        o_ref[...] = (o * lax.rsqrt(ms + NORM_EPS) * sg_ref[...] * (1.0 - lam_init)).astype(o_ref.dtype)

    is_ctx = pl.program_id(2) < t_ctx // tq

    @pl.when(is_ctx)
    def _():
        attend(k_ref[0:t_ctx, :], v_ref[0:t_ctx, :])

    @pl.when(jnp.logical_not(is_ctx))
    def _():
        attend(k_ref[...], v_ref[...])


def _diff_attention(qkv, lam_vec, sub_gain, lam_init, batch, t_ctx, tq):
    _, n, width = qkv.shape
    heads = width // LANES
    seq = n // batch
    nq = seq // tq
    return pl.pallas_call(
        functools.partial(_diff_attn_kernel, lam_init=lam_init, t_ctx=t_ctx),
        grid=(batch, heads, nq),
        in_specs=[
            pl.BlockSpec((None, tq, LANES), lambda b, h, i: (0, b * nq + i, h)),
            pl.BlockSpec((None, seq, LANES), lambda b, h, i: (1, b, h)),
            pl.BlockSpec((None, seq, LANES), lambda b, h, i: (2, b, h)),
            pl.BlockSpec((4, HEAD_DIM), lambda b, h, i: (0, 0)),
            pl.BlockSpec((1, LANES), lambda b, h, i: (0, 0)),
        ],
        out_specs=pl.BlockSpec((tq, LANES), lambda b, h, i: (b * nq + i, h)),
        out_shape=jax.ShapeDtypeStruct((n, width), BF16),
        compiler_params=_cparams(("arbitrary", "arbitrary", "arbitrary")),
        name="diff_attn",
    )(qkv, qkv, qkv, lam_vec, sub_gain.reshape(1, LANES))


def _peer_scores_kernel(q_ref, keys_ref, o_ref):
    o_ref[...] = lax.dot_general(keys_ref[...], q_ref[...], (((1,), (1,)), ((), ())),
                                 precision=lax.Precision.HIGHEST, preferred_element_type=F32)


def _peer_scores(q, keys, tt):
    n = q.shape[0]
    hp, nk, half = keys.shape
    return pl.pallas_call(
        _peer_scores_kernel,
        grid=(n // tt, hp),
        in_specs=[
            pl.BlockSpec((tt, half), lambda i, j: (i, j)),
            pl.BlockSpec((None, nk, half), lambda i, j: (j, 0, 0)),
        ],
        out_specs=pl.BlockSpec((None, nk, tt), lambda i, j: (j, 0, i)),
        out_shape=jax.ShapeDtypeStruct((hp, nk, n), F32),
        compiler_params=_cparams(("arbitrary", "arbitrary")),
        name="peer_scores",
    )(q, keys)


SPLIT_B = 4


def _peer_select_kernel(s_ref, e1_ref, bi_ref, e2_ref, r2_ref, rank_ref, val_ref, sel_ref):
    nk, tl = s_ref.shape[1:]
    k = PEER_TOPK
    neg = -jnp.inf
    unranked = float(2 * k)

    def top(s, exact):
        key = lax.broadcasted_iota(jnp.int32, s.shape, 0)
        work = s
        rank = jnp.full(s.shape, unranked, F32)
        vals = []
        for r in range(k):
            m = jnp.max(work, axis=0, keepdims=True)
            hit = work == m
            if exact:
                hit = key == jnp.min(jnp.where(hit, key, nk), axis=0, keepdims=True)
            work = jnp.where(hit, neg, work)
            rank = jnp.where(hit, float(r), rank)
            vals.append(m)
        removed = jnp.sum(jnp.where(rank < unranked, 1.0, 0.0), axis=0, keepdims=True)
        return jnp.concatenate(vals, axis=0), rank, removed

    def stage1(exact):
        removed = jnp.zeros((1, tl), F32)
        for p in range(2):
            vals, rank, rem = top(s_ref[p], exact)
            val_ref[p] = vals
            rank_ref[p] = rank
            removed = jnp.maximum(removed, rem)
        return jnp.max(removed)

    sub = lax.broadcasted_iota(jnp.int32, (k, tl), 0)
    n_a = k // (SPLIT_B + 1)

    def stage2(exact):
        v1 = val_ref[0]
        v2 = val_ref[1]
        work, flat = [], []
        for b in range(SPLIT_B):
            ok = sub <= k // (b + 1) - 1
            work.append(jnp.where(ok, v1 + v2[b:b + 1], neg))
            flat.append(jnp.where(ok, sub * k + b, -1))
        for a in range(n_a):
            ok = (sub >= SPLIT_B) & (sub <= k // (a + 1) - 1)
            work.append(jnp.where(ok, v1[a:a + 1] + v2, neg))
            flat.append(jnp.where(ok, sub + a * k, -1))
        count = [jnp.zeros((k, tl), F32) for _ in work]
        top_sum = v1[0:1] + v2[0:1]
        z = jnp.zeros((1, tl), F32)
        for _ in range(k):
            m = jnp.max(functools.reduce(jnp.maximum, work), axis=0, keepdims=True)
            z = z + jnp.exp(m - top_sum)
            if exact:
                cand = functools.reduce(jnp.minimum, [jnp.where(w == m, f, k * k) for w, f in zip(work, flat)])
                idx = jnp.min(cand, axis=0, keepdims=True)
            for n in range(len(work)):
                hit = (flat[n] == idx) if exact else (work[n] == m)
                work[n] = jnp.where(hit, neg, work[n])
                count[n] = count[n] + jnp.where(hit, 1.0, 0.0)
        per_a = functools.reduce(jnp.add, count[:SPLIT_B])
        for a in range(n_a):
            extra = jnp.sum(count[SPLIT_B + a], axis=0, keepdims=True)
            per_a = per_a + jnp.where(sub == a, extra, 0.0)
        sel_ref[0:k, :] = per_a
        sel_ref[k:k + 1, :] = z
        return jnp.max(jnp.sum(per_a, axis=0, keepdims=True))

    @pl.when(stage1(False) > float(k))
    def _():
        stage1(True)

    @pl.when(stage2(False) > float(k))
    def _():
        stage2(True)

    s1 = s_ref[0]
    rank1 = rank_ref[0]
    bi = jnp.zeros(s1.shape, F32)
    for a in range(k):
        bi = jnp.where(rank1 == float(a), sel_ref[a:a + 1, :], bi)
    bi_ref[...] = bi
    e1_ref[...] = jnp.exp(s1 - val_ref[0, 0:1, :]) / sel_ref[k:k + 1, :]
    e2_ref[...] = jnp.exp(s_ref[1] - val_ref[1, 0:1, :]).astype(e2_ref.dtype)
    r2_ref[...] = rank_ref[1].astype(r2_ref.dtype)


def _peer_select(scores, heads, tl):
    hp, nk, n = scores.shape
    k = PEER_TOPK
    wide = jax.ShapeDtypeStruct((heads, nk, n), F32)
    narrow = jax.ShapeDtypeStruct((heads, nk, n), BF16)
    spec = pl.BlockSpec((None, nk, tl), lambda t, h: (h, 0, t))
    return pl.pallas_call(
        _peer_select_kernel,
        grid=(n // tl, heads),
        in_specs=[pl.BlockSpec((None, 2, nk, tl), lambda t, h: (h, 0, 0, t))],
        out_specs=[spec] * 4,
        out_shape=[wide, wide, narrow, narrow],
        scratch_shapes=[pltpu.VMEM((2, nk, tl), F32), pltpu.VMEM((2, k, tl), F32),
                        pltpu.VMEM((k + 8, tl), F32)],
        compiler_params=_cparams(("arbitrary", "arbitrary")),
        name="peer_select",
    )(scores.reshape(heads, 2, nk, n))


def _peer_dense_kernel(h_ref, u_ref, v_ref, e1_ref, bi_ref, e2_ref, r2_ref, o_ref, a_ref, m_ref, *, nk, chunk):
    e = pl.program_id(1)
    heads = e1_ref.shape[0]
    et = u_ref.shape[0]
    tl = h_ref.shape[0]
    sub = 16

    @pl.when(e == 0)
    def _():
        o_ref[...] = jnp.zeros_like(o_ref)

    zero = jnp.zeros((sub, LANES), BF16)
    total = None
    for c in range(et // chunk):
        a_ref[c] = lax.dot_general(u_ref[c * chunk:(c + 1) * chunk, :], h_ref[...], (((1,), (1,)), ((), ())),
                                   preferred_element_type=F32)
    for c in range(et // chunk):
        for ii in range(chunk // nk):
            i = (e * et + c * chunk) // nk + ii
            for lt in range(tl // LANES):
                lanes = slice(lt * LANES, (lt + 1) * LANES)

                def row(ref, hh):
                    return jnp.broadcast_to(ref[hh, pl.ds(i, 1), :][:, lanes], (sub, LANES)).astype(BF16)

                e1 = [row(e1_ref, hh) for hh in range(heads)]
                bi = [row(bi_ref, hh) for hh in range(heads)]
                for jb in range(nk // sub):
                    js = slice(jb * sub, (jb + 1) * sub)
                    g = None
                    for hh in range(heads):
                        term = jnp.where(r2_ref[hh, js, lanes] < bi[hh], e2_ref[hh, js, lanes] * e1[hh], zero)
                        g = term if g is None else g + term
                    rows = slice(ii * nk + jb * sub, ii * nk + (jb + 1) * sub)
                    a = a_ref[c, rows, lanes]
                    act = 0.5 * a * (1.0 + lax.erf(a * (2.0 ** -0.5)))
                    m_ref[c, rows, lanes] = g * act.astype(BF16)
        contrib = jnp.dot(m_ref[c].T, v_ref[c * chunk:(c + 1) * chunk, :], preferred_element_type=F32)
        total = contrib if total is None else total + contrib
    o_ref[...] += total


def _peer_dense(h, u, v, e1, bi, e2, r2, tl, et, chunk):
    n, d = h.shape
    ne = u.shape[0]
    heads, nk, _ = e1.shape
    fac = pl.BlockSpec((heads, nk, tl), lambda t, e: (0, 0, t))
    return pl.pallas_call(
        functools.partial(_peer_dense_kernel, nk=nk, chunk=chunk),
        grid=(n // tl, ne // et),
        in_specs=[
            pl.BlockSpec((tl, d), lambda t, e: (t, 0)),
            pl.BlockSpec((et, d), lambda t, e: (e, 0)),
            pl.BlockSpec((et, d), lambda t, e: (e, 0)),
            fac, fac, fac, fac,
        ],
        out_specs=pl.BlockSpec((tl, d), lambda t, e: (t, 0)),
        out_shape=jax.ShapeDtypeStruct((n, d), F32),
        scratch_shapes=[pltpu.VMEM((et // chunk, chunk, tl), F32), pltpu.VMEM((et // chunk, chunk, tl), BF16)],
        compiler_params=_cparams(("arbitrary", "arbitrary")),
        name="peer_dense",
    )(h, u, v, e1, bi, e2, r2)


def _gated_add_kernel(gid_ref, x_ref, y_ref, gate_ref, o_ref):
    del gid_ref
    o_ref[...] = x_ref[...] + gate_ref[...] * y_ref[...]


def _gated_add(x, y, gid, gate, tm):
    n, d = x.shape
    spec = pl.BlockSpec((tm, d), lambda i, g: (i, 0))
    return pl.pallas_call(
        _gated_add_kernel,
        grid_spec=pltpu.PrefetchScalarGridSpec(
            num_scalar_prefetch=1,
            grid=(n // tm,),
            in_specs=[spec, spec, pl.BlockSpec((None, 1, d), lambda i, g: (g[i], 0, 0))],
            out_specs=spec,
        ),
        out_shape=jax.ShapeDtypeStruct((n, d), F32),
        compiler_params=_cparams(("arbitrary",)),
        name="gated_add",
    )(gid, x, y, gate)


def _rope_tables(t_lat, t_ctx, batch):
    rows = t_lat // GRID_W
    row = jnp.repeat(jnp.arange(rows, dtype=jnp.int32), GRID_W)
    col = jnp.tile(jnp.arange(GRID_W, dtype=jnp.int32), rows)
    nfreq = HEAD_DIM // 4
    inv_freq = ROPE_THETA ** (-jnp.arange(nfreq, dtype=F32) / nfreq)
    ang_r = row.astype(F32)[:, None] * inv_freq
    ang_c = col.astype(F32)[:, None] * inv_freq
    cos = jnp.concatenate([jnp.cos(ang_r)] * 2 + [jnp.cos(ang_c)] * 2, axis=1)
    sin = jnp.concatenate([-jnp.sin(ang_r), jnp.sin(ang_r), -jnp.sin(ang_c), jnp.sin(ang_c)], axis=1)
    reps = LANES // HEAD_DIM
    cos = jnp.concatenate([jnp.ones((t_ctx, LANES), F32), jnp.tile(cos, (1, reps))], axis=0)
    sin = jnp.concatenate([jnp.zeros((t_ctx, LANES), F32), jnp.tile(sin, (1, reps))], axis=0)
    return jnp.tile(cos, (batch, 1)), jnp.tile(sin, (batch, 1))


def kernel(x, c, ctx, c_ctx, ada_w, ada_b, norm_gain, w_in, conv_w, rw_w0, rw_w_up, rw_a0, rw_a_up,
           rw_g_up, rw_k_k, rw_k_a, rw_r_k, rw_gn_gain, rw_gn_bias, da_q_gain, da_k_gain, da_lambda,
           da_sub_gain, w_out, peer_wq, peer_keys, peer_u, peer_v):
    batch, t_lat, d = x.shape
    t_ctx = ctx.shape[1]
    seq = t_ctx + t_lat
    n = batch * seq
    depth = ada_w.shape[0]
    rw = rw_k_k.shape[1]
    lora_w = rw_w_up.shape[2] + rw_a_up.shape[2] + rw_g_up.shape[1]
    da_w = w_out.shape[1] - rw
    p_heads, _, nk, half = peer_keys.shape[1:]
    assert 2 * batch * (rw // HEAD_DIM) == LANES, "the recurrence kernel fills the lanes with (half, batch, head)"

    tm = _tile(math.gcd(t_lat, t_ctx), 256)
    tile_pos = (jnp.arange(n // tm, dtype=jnp.int32) * tm) % seq
    gid = jnp.where(tile_pos < t_ctx, batch, jnp.arange(n // tm, dtype=jnp.int32) * tm // seq)
    first = ((tile_pos == 0) | (tile_pos == t_ctx)).astype(jnp.int32)
    last = ((tile_pos + tm == t_ctx) | (tile_pos + tm == seq)).astype(jnp.int32)

    xs = jnp.concatenate([ctx, x], axis=1).reshape(n, d)
    c8 = jnp.concatenate([c, c_ctx[None], jnp.zeros((8 - batch - 1, d), F32)], axis=0)
    mods = _modulation(c8, ada_w, ada_b)[:, :batch + 1].reshape(depth, batch + 1, 6, 1, d)
    cos_t, sin_t = _rope_tables(t_lat, t_ctx, batch)

    cols = jnp.concatenate([jnp.arange(3 * rw), jnp.arange(3 * rw + lora_w, w_in.shape[2]),
                            jnp.arange(3 * rw, 3 * rw + lora_w)])
    tn_in = _tile(w_in.shape[2], 1280, LANES)
    ts = _tile(tm, 128)

    for l in range(depth):
        lam_init = 0.8 - 0.6 * math.exp(-0.3 * l)
        mod = [mods[l, :, j] for j in range(6)]

        p = _norm_proj_wide(xs, gid, norm_gain[l, 0], mod[0], mod[1], w_in[l][:, cols].astype(BF16), tm,
                            _tile(n, 1024), tn_in)
        q8, v, bonus, gate = _rwkv_prep(
            p, first, last, conv_w[l], rw_w0[l], rw_w_up[l], rw_a0[l], rw_a_up[l], rw_g_up[l],
            rw_k_k[l], rw_k_a[l], rw_r_k[l].reshape(-1), tm)
        qs, vs = _to_streams(q8, v, batch, ts)
        yf, yb = _rwkv_scan(qs, vs, t_ctx, tc=_tile(math.gcd(t_lat, t_ctx), 32))
        rw_out = _rwkv_finish(yf, yb, bonus, gate, rw_gn_gain[l], rw_gn_bias[l], batch, ts)

        qkv = _da_prep(p, cos_t, sin_t, da_q_gain[l], da_k_gain[l], 3 * rw, da_w, tm)
        da_out = _diff_attention(qkv, da_lambda[l], da_sub_gain[l], lam_init, batch, t_ctx, tm)
        xs = _gated_proj(rw_out, da_out, w_out[l].astype(BF16), xs, gid, mod[2], tm)

        q, h = _norm_proj(xs, gid, norm_gain[l, 1], mod[3], mod[4], peer_wq[l].astype(BF16), tm,
                          _tile(peer_wq.shape[2], 2048, LANES), emit_h=True)
        scores = _peer_scores(q, peer_keys[l].reshape(2 * p_heads, nk, half), _tile(n, 1024, LANES))
        e1, bi, e2, r2 = _peer_select(scores, p_heads, _tile(n, 512, LANES))
        out = _peer_dense(h, peer_u[l].astype(BF16), peer_v[l].astype(BF16), e1, bi, e2, r2,
                          tl=_tile(n, 512, LANES), et=4 * nk, chunk=2 * nk)
        xs = _gated_add(xs, out, gid, mod[5], tm)

    return xs.reshape(batch, seq, d)[:, t_ctx:]
```

```python
import functools
import math

import jax
import jax.numpy as jnp
from jax import lax
from jax.experimental import pallas as pl
from jax.experimental.pallas import tpu as pltpu

F32 = jnp.float32
BF16 = jnp.bfloat16

HEAD_DIM = 64
LANES = 128
GRID_W = 64
ROPE_THETA = 10000.0
NORM_EPS = 1e-6
GN_EPS = 64e-5
L2_EPS = 1e-12
PEER_TOPK = 16
VMEM_LIMIT = 56 * 1024 * 1024


def _cparams(sem):
    return pltpu.CompilerParams(dimension_semantics=sem, vmem_limit_bytes=VMEM_LIMIT)


def _mm(a, b):
    return jnp.dot(a.astype(BF16), b.astype(BF16), preferred_element_type=F32)


def _segment_ones():
    r = lax.broadcasted_iota(jnp.int32, (LANES, LANES), 0) >> 6
    c = lax.broadcasted_iota(jnp.int32, (LANES, LANES), 1) >> 6
    return jnp.where(r == c, 1.0, 0.0).astype(BF16)


def _segsum(x, ones_bd):
    hi = x.astype(BF16)
    lo = (x - hi.astype(F32)).astype(BF16)
    return (jnp.dot(hi, ones_bd, preferred_element_type=F32)
            + jnp.dot(lo, ones_bd, preferred_element_type=F32))


def _tile(n, pref, align=8):
    for t in range(min(n, pref) // align * align, 0, -align):
        if n % t == 0:
            return t
    return n


def _mod_kernel(c_ref, w_ref, b_ref, o_ref):
    cv = c_ref[...]
    act = cv * jax.nn.sigmoid(cv)
    o_ref[...] = _mm(act, w_ref[...]) + b_ref[...]


def _modulation(c8, ada_w, ada_b):
    depth, d, n6 = ada_w.shape
    tn = _tile(n6, 1536)
    return pl.pallas_call(
        _mod_kernel,
        grid=(depth, n6 // tn),
        in_specs=[
            pl.BlockSpec((8, d), lambda l, j: (0, 0)),
            pl.BlockSpec((None, d, tn), lambda l, j: (l, 0, j)),
            pl.BlockSpec((None, 1, tn), lambda l, j: (l, 0, j)),
        ],
        out_specs=pl.BlockSpec((None, 8, tn), lambda l, j: (l, 0, j)),
        out_shape=jax.ShapeDtypeStruct((depth, 8, n6), F32),
        compiler_params=_cparams(("arbitrary", "arbitrary")),
        name="modulation",
    )(c8, ada_w, ada_b.reshape(depth, 1, n6))


def _norm_proj_kernel(gid_ref, x_ref, gain_ref, shift_ref, scale_ref, w_ref, o_ref, *rest, emit_h):
    del gid_ref
    hs_ref = rest[-1]

    @pl.when(pl.program_id(1) == 0)
    def _():
        x = x_ref[...]
        ms = jnp.mean(x * x, axis=-1, keepdims=True)
        y = x * lax.rsqrt(ms + NORM_EPS) * gain_ref[...]
        h = y * (1.0 + scale_ref[...]) + shift_ref[...]
        hs_ref[...] = h.astype(BF16)
        if emit_h:
            rest[0][...] = h.astype(BF16)

    o_ref[...] = jnp.dot(hs_ref[...], w_ref[...], preferred_element_type=F32)


def _norm_proj(x, gid, gain, shift, scale, w, tm, tn, emit_h=False):
    n, d = x.shape
    nout = w.shape[1]
    out_shape = [jax.ShapeDtypeStruct((n, nout), F32)]
    out_specs = [pl.BlockSpec((tm, tn), lambda i, j, g: (i, j))]
    if emit_h:
        out_shape.append(jax.ShapeDtypeStruct((n, d), BF16))
        out_specs.append(pl.BlockSpec((tm, d), lambda i, j, g: (i, 0)))
    res = pl.pallas_call(
        functools.partial(_norm_proj_kernel, emit_h=emit_h),
        grid_spec=pltpu.PrefetchScalarGridSpec(
            num_scalar_prefetch=1,
            grid=(n // tm, nout // tn),
            in_specs=[
                pl.BlockSpec((tm, d), lambda i, j, g: (i, 0)),
                pl.BlockSpec((1, d), lambda i, j, g: (0, 0)),
                pl.BlockSpec((None, 1, d), lambda i, j, g: (g[i], 0, 0)),
                pl.BlockSpec((None, 1, d), lambda i, j, g: (g[i], 0, 0)),
                pl.BlockSpec((d, tn), lambda i, j, g: (0, j)),
            ],
            out_specs=out_specs,
            scratch_shapes=[pltpu.VMEM((tm, d), BF16)],
        ),
        out_shape=out_shape,
        compiler_params=_cparams(("arbitrary", "arbitrary")),
        name="norm_proj",
    )(gid, x, gain.reshape(1, d), shift, scale, w)
    return res if emit_h else res[0]


def _norm_mod_kernel(gid_ref, x_ref, gain_ref, shift_ref, scale_ref, o_ref):
    del gid_ref
    x = x_ref[...]
    ms = jnp.mean(x * x, axis=-1, keepdims=True)
    y = x * lax.rsqrt(ms + NORM_EPS) * gain_ref[...]
    o_ref[...] = (y * (1.0 + scale_ref[...]) + shift_ref[...]).astype(BF16)


def _matmul_kernel(h_ref, w_ref, o_ref):
    o_ref[...] = jnp.dot(h_ref[...], w_ref[...], preferred_element_type=F32)


def _norm_proj_wide(x, gid, gain, shift, scale, w, tm, tm_mm, tn):
    n, d = x.shape
    nout = w.shape[1]
    h = pl.pallas_call(
        _norm_mod_kernel,
        grid_spec=pltpu.PrefetchScalarGridSpec(
            num_scalar_prefetch=1,
            grid=(n // tm,),
            in_specs=[
                pl.BlockSpec((tm, d), lambda i, g: (i, 0)),
                pl.BlockSpec((1, d), lambda i, g: (0, 0)),
                pl.BlockSpec((None, 1, d), lambda i, g: (g[i], 0, 0)),
                pl.BlockSpec((None, 1, d), lambda i, g: (g[i], 0, 0)),
            ],
            out_specs=pl.BlockSpec((tm, d), lambda i, g: (i, 0)),
        ),
        out_shape=jax.ShapeDtypeStruct((n, d), BF16),
        compiler_params=_cparams(("arbitrary",)),
        name="norm_mod",
    )(gid, x, gain.reshape(1, d), shift, scale)
    return pl.pallas_call(
        _matmul_kernel,
        grid=(nout // tn, n // tm_mm),
        in_specs=[
            pl.BlockSpec((tm_mm, d), lambda j, i: (i, 0)),
            pl.BlockSpec((d, tn), lambda j, i: (0, j)),
        ],
        out_specs=pl.BlockSpec((tm_mm, tn), lambda j, i: (i, j)),
        out_shape=jax.ShapeDtypeStruct((n, nout), F32),
        compiler_params=_cparams(("arbitrary", "arbitrary")),
        name="proj_matmul",
    )(h, w)


def _gated_proj_kernel(gid_ref, y1_ref, y2_ref, w1_ref, w2_ref, x_ref, gate_ref, o_ref):
    del gid_ref
    mix = (jnp.dot(y1_ref[...], w1_ref[...], preferred_element_type=F32)
           + jnp.dot(y2_ref[...], w2_ref[...], preferred_element_type=F32))
    o_ref[...] = x_ref[...] + gate_ref[...] * mix


def _gated_proj(y1, y2, w, x, gid, gate, tm):
    n, k1 = y1.shape
    k2 = y2.shape[1]
    d = w.shape[1]
    return pl.pallas_call(
        _gated_proj_kernel,
        grid_spec=pltpu.PrefetchScalarGridSpec(
            num_scalar_prefetch=1,
            grid=(n // tm,),
            in_specs=[
                pl.BlockSpec((tm, k1), lambda i, g: (i, 0)),
                pl.BlockSpec((tm, k2), lambda i, g: (i, 0)),
                pl.BlockSpec((k1, d), lambda i, g: (0, 0)),
                pl.BlockSpec((k2, d), lambda i, g: (k1 // k2, 0)),
                pl.BlockSpec((tm, d), lambda i, g: (i, 0)),
                pl.BlockSpec((None, 1, d), lambda i, g: (g[i], 0, 0)),
            ],
            out_specs=pl.BlockSpec((tm, d), lambda i, g: (i, 0)),
        ),
        out_shape=jax.ShapeDtypeStruct((n, d), F32),
        compiler_params=_cparams(("arbitrary",)),
        name="gated_proj",
    )(gid, y1, y2, w, w, x, gate)


def _rwkv_prep_kernel(first_ref, last_ref, x_ref, prev_ref, next_ref, tail_ref, cw_ref,
                      w0_ref, wup_ref, a0_ref, aup_ref, gup_ref, kk_ref, ka_ref, rk_ref,
                      q_out, v_out, bonus_out, gate_out, *, rw, lw, la):
    i = pl.program_id(0)
    tm = x_ref.shape[0]
    x = x_ref[...]
    row = lax.broadcasted_iota(jnp.int32, x.shape, 0)
    keep_prev = jnp.where(first_ref[i] == 1, 0.0, 1.0)
    keep_next = jnp.where(last_ref[i] == 1, 0.0, 1.0)
    halo_prev = prev_ref[7:8, :] * keep_prev
    halo_next = next_ref[0:1, :] * keep_next
    x_prev = jnp.where(row == 0, halo_prev, pltpu.roll(x, 1, axis=0))
    x_next = jnp.where(row == tm - 1, halo_next, pltpu.roll(x, tm - 1, axis=0))
    conv = x_prev * cw_ref[0:1, :] + x * cw_ref[1:2, :] + x_next * cw_ref[2:3, :]
    r = conv[:, 0:rw]
    k = conv[:, rw:2 * rw]
    v = conv[:, 2 * rw:3 * rw]
    tail = tail_ref[...]
    w_lo = jnp.tanh(tail[:, 0:lw])
    a_lo = tail[:, lw:lw + la]
    g_lo = jax.nn.sigmoid(tail[:, lw + la:])
    ones_bd = _segment_ones()

    def seg(t):
        return jnp.concatenate(
            [_segsum(t[:, c * LANES:(c + 1) * LANES], ones_bd) for c in range(rw // LANES)], axis=1)

    kap = k * kk_ref[...]
    nrm = jnp.sqrt(seg(kap * kap))
    kap = kap / jnp.maximum(nrm, L2_EPS)
    q_out[0] = r
    q_out[1] = kap
    v_out[...] = v
    gate_out[...] = _mm(g_lo, gup_ref[...])
    bonus = jnp.zeros_like(r)
    for d in range(2):
        z = w0_ref[d:d + 1, :] + _mm(w_lo, wup_ref[d])
        log_w = -jax.nn.softplus(-z) - 0.5
        q_out[2 + d] = jnp.exp(-jnp.exp(log_w))
        a = jax.nn.sigmoid(a0_ref[d:d + 1, :] + _mm(a_lo, aup_ref[d]))
        k_mod = k * (1.0 + (a - 1.0) * ka_ref[...])
        q_out[4 + d] = k_mod
        q_out[6 + d] = kap * a
        bonus = bonus + seg(r * k_mod * rk_ref[...]) * v
    bonus_out[...] = bonus


def _rwkv_prep(p, first, last, conv_w, w0, w_up, a0, a_up, g_up, k_k, k_a, r_k, tm):
    n = p.shape[0]
    rw = k_k.shape[0]
    lw, la, lg = w_up.shape[1], a_up.shape[1], g_up.shape[0]
    tail_w = lw + la + lg
    tail_blk = (p.shape[1] - tail_w) // tail_w
    nb8 = n // 8
    row2 = lambda i, f, l: (0, 0)
    row3 = lambda i, f, l: (0, 0, 0)
    one = jax.ShapeDtypeStruct((n, rw), F32)
    one_spec = pl.BlockSpec((tm, rw), lambda i, f, l: (i, 0))
    return pl.pallas_call(
        functools.partial(_rwkv_prep_kernel, rw=rw, lw=lw, la=la),
        grid_spec=pltpu.PrefetchScalarGridSpec(
            num_scalar_prefetch=2,
            grid=(n // tm,),
            in_specs=[
                pl.BlockSpec((tm, 3 * rw), lambda i, f, l: (i, 0)),
                pl.BlockSpec((8, 3 * rw), lambda i, f, l: (jnp.maximum(i * (tm // 8) - 1, 0), 0)),
                pl.BlockSpec((8, 3 * rw), lambda i, f, l: (jnp.minimum((i + 1) * (tm // 8), nb8 - 1), 0)),
                pl.BlockSpec((tm, tail_w), lambda i, f, l: (i, tail_blk)),
                pl.BlockSpec((3, 3 * rw), row2),
                pl.BlockSpec((2, rw), row2),
                pl.BlockSpec((2, lw, rw), row3),
                pl.BlockSpec((2, rw), row2),
                pl.BlockSpec((2, la, rw), row3),
                pl.BlockSpec((lg, rw), row2),
                pl.BlockSpec((1, rw), row2),
                pl.BlockSpec((1, rw), row2),
                pl.BlockSpec((1, rw), row2),
            ],
            out_specs=[pl.BlockSpec((8, tm, rw), lambda i, f, l: (0, i, 0)), one_spec, one_spec, one_spec],
        ),
        out_shape=[jax.ShapeDtypeStruct((8, n, rw), F32), one, one, one],
        compiler_params=_cparams(("arbitrary",)),
        name="rwkv_prep",
    )(first, last, p, p, p, p, conv_w, w0, w_up, a0, a_up, g_up,
      k_k.reshape(1, rw), k_a.reshape(1, rw), r_k.reshape(1, rw))


def _to_streams_kernel(x_ref, o_ref, xt_ref, *, split_value):
    batch, tm, rw = x_ref.shape
    heads = rw // HEAD_DIM
    for b in range(batch):
        xt_ref[b] = x_ref[b].T
    rows = HEAD_DIM // 2 if split_value else HEAD_DIM
    for j in range(rows):
        second = j + HEAD_DIM // 2 if split_value else j
        parts = [xt_ref[b, pl.ds(j, heads, stride=HEAD_DIM), :] for b in range(batch)]
        parts += [xt_ref[b, pl.ds(second, heads, stride=HEAD_DIM), :] for b in range(batch)]
        slab = jnp.concatenate(parts, axis=0).T
        if split_value:
            o_ref[:, j, :] = slab
        else:
            o_ref[j] = slab


def _to_streams(q, v, batch, tm):
    nq, n, rw = q.shape
    seq = n // batch
    scratch = [pltpu.VMEM((batch, rw, tm), F32)]
    qs = pl.pallas_call(
        functools.partial(_to_streams_kernel, split_value=False),
        grid=(nq, seq // tm),
        in_specs=[pl.BlockSpec((None, batch, tm, rw), lambda a, i: (a, 0, i, 0))],
        out_specs=pl.BlockSpec((None, HEAD_DIM, tm, LANES), lambda a, i: (a, 0, i, 0)),
        out_shape=jax.ShapeDtypeStruct((nq, HEAD_DIM, seq, LANES), F32),
        scratch_shapes=scratch,
        compiler_params=_cparams(("arbitrary", "arbitrary")),
        name="to_streams",
    )(q.reshape(nq, batch, seq, rw))
    vs = pl.pallas_call(
        functools.partial(_to_streams_kernel, split_value=True),
        grid=(seq // tm,),
        in_specs=[pl.BlockSpec((batch, tm, rw), lambda i: (0, i, 0))],
        out_specs=pl.BlockSpec((tm, HEAD_DIM // 2, LANES), lambda i: (i, 0, 0)),
        out_shape=jax.ShapeDtypeStruct((seq, HEAD_DIM // 2, LANES), F32),
        scratch_shapes=scratch,
        compiler_params=_cparams(("arbitrary",)),
        name="to_streams_v",
    )(v.reshape(batch, seq, rw))
    return qs, vs


def _rwkv_scan_kernel(rf, kapf, wf, kf, bf, vf, rb, kapb, wb, kb, bb, vb, yf, yb, s_ref):
    hd, tc, ns = rf.shape
    half = vf.shape[1]

    @pl.when(pl.program_id(0) == 0)
    def _():
        s_ref[...] = jnp.zeros_like(s_ref)

    dirs = ((rf, kapf, wf, kf, bf, vf, yf), (rb, kapb, wb, kb, bb, vb, yb))

    def state_times_kappa(d, t):
        acc = jnp.zeros((half, ns), F32)
        for kk in range(hd):
            acc = acc + s_ref[d, kk] * dirs[d][1][kk, pl.ds(t, 1), :]
        return acc

    def update(d, t, t_next, sa):
        r_ref, kap_ref, w_ref, k_ref, b_ref, v_ref, y_ref = dirs[d]
        vv = v_ref[t]
        y = jnp.zeros((half, ns), F32)
        sa_next = jnp.zeros((half, ns), F32)
        for kk in range(hd):
            s_new = (s_ref[d, kk] * w_ref[kk, pl.ds(t, 1), :]
                     + (vv * k_ref[kk, pl.ds(t, 1), :] - sa * b_ref[kk, pl.ds(t, 1), :]))
            s_ref[d, kk] = s_new
            y = y + s_new * r_ref[kk, pl.ds(t, 1), :]
            sa_next = sa_next + s_new * kap_ref[kk, pl.ds(t_next, 1), :]
        y_ref[t] = y
        return sa_next

    def step(j, carry):
        sa_f, sa_b = carry
        sa_f = update(0, j, jnp.minimum(j + 1, tc - 1), sa_f)
        tb = tc - 1 - j
        sa_b = update(1, tb, jnp.maximum(tb - 1, 0), sa_b)
        return sa_f, sa_b

    lax.fori_loop(0, tc, step, (state_times_kappa(0, 0), state_times_kappa(1, tc - 1)))


def _rwkv_scan(qs, vs, t_ctx, tc):
    _, hd, seq, ns = qs.shape
    half = vs.shape[1]
    nb = seq // tc
    nbc = t_ctx // tc

    def back_block(i):
        return jnp.where(i < nbc, nbc - 1 - i, nb - 1 - (i - nbc))

    def qspec(a, back):
        if back:
            return pl.BlockSpec((None, hd, tc, ns), lambda i: (a, 0, back_block(i), 0))
        return pl.BlockSpec((None, hd, tc, ns), lambda i: (a, 0, i, 0))

    def vspec(back):
        if back:
            return pl.BlockSpec((tc, half, ns), lambda i: (back_block(i), 0, 0))
        return pl.BlockSpec((tc, half, ns), lambda i: (i, 0, 0))

    in_specs = ([qspec(a, False) for a in (0, 1, 2, 4, 6)] + [vspec(False)]
                + [qspec(a, True) for a in (0, 1, 3, 5, 7)] + [vspec(True)])
    out = jax.ShapeDtypeStruct((seq, half, ns), F32)
    return pl.pallas_call(
        _rwkv_scan_kernel,
        grid=(nb,),
        in_specs=in_specs,
        out_specs=[vspec(False), vspec(True)],
        out_shape=[out, out],
        scratch_shapes=[pltpu.VMEM((2, hd, half, ns), F32)],
        compiler_params=_cparams(("arbitrary",)),
        name="rwkv_scan",
    )(*([qs] * 5 + [vs] + [qs] * 5 + [vs]))


def _rwkv_finish_kernel(yf_ref, yb_ref, bonus_ref, gate_ref, gain_ref, bias_ref, o_ref, xt_ref):
    batch, tm, rw = bonus_ref.shape
    heads = rw // HEAD_DIM
    half = yf_ref.shape[1]
    for j in range(half):
        slab = (yf_ref[:, j, :] + yb_ref[:, j, :]).T
        for g in range(2 * batch):
            xt_ref[g % batch, pl.ds(j + half * (g // batch), heads, stride=HEAD_DIM), :] = (
                slab[g * heads:(g + 1) * heads])
    ones_bd = _segment_ones()
    for b in range(batch):
        y = xt_ref[b].T
        outs = []
        for c in range(rw // LANES):
            yc = y[:, c * LANES:(c + 1) * LANES]
            mu = _segsum(yc, ones_bd) * (1.0 / HEAD_DIM)
            dev = yc - mu
            var = _segsum(dev * dev, ones_bd) * (1.0 / HEAD_DIM)
            outs.append(dev * lax.rsqrt(var + GN_EPS))
        yn = jnp.concatenate(outs, axis=1)
        o_ref[b] = ((yn * gain_ref[...] + bias_ref[...] + bonus_ref[b]) * gate_ref[b]).astype(o_ref.dtype)


def _rwkv_finish(yf, yb, bonus, gate, gain, bias, batch, tm):
    seq, half, ns = yf.shape
    n, rw = bonus.shape
    yspec = pl.BlockSpec((tm, half, ns), lambda i: (i, 0, 0))
    spec = pl.BlockSpec((batch, tm, rw), lambda i: (0, i, 0))
    vec = pl.BlockSpec((1, rw), lambda i: (0, 0))
    out = pl.pallas_call(
        _rwkv_finish_kernel,
        grid=(seq // tm,),
        in_specs=[yspec, yspec, spec, spec, vec, vec],
        out_specs=spec,
        out_shape=jax.ShapeDtypeStruct((batch, seq, rw), BF16),
        scratch_shapes=[pltpu.VMEM((batch, rw, tm), F32)],
        compiler_params=_cparams(("arbitrary",)),
        name="rwkv_finish",
    )(yf, yb, bonus.reshape(batch, seq, rw), gate.reshape(batch, seq, rw),
      gain.reshape(1, rw), bias.reshape(1, rw))
    return out.reshape(n, rw)


def _da_prep_kernel(x_ref, cos_ref, sin_ref, qg_ref, kg_ref, o_ref):
    s = pl.program_id(1)

    @pl.when(s == 2)
    def _():
        o_ref[...] = x_ref[...].astype(BF16)

    @pl.when(s < 2)
    def _():
        ones_bd = _segment_ones()
        gain = jnp.where(s == 0, qg_ref[...] * (HEAD_DIM ** -0.5), kg_ref[...])
        cos = cos_ref[...]
        sin = sin_ref[...]
        lane = lax.broadcasted_iota(jnp.int32, cos.shape, 1)
        even = ((lane >> 4) & 1) == 0
        x = x_ref[...]
        for c in range(x.shape[1] // LANES):
            xc = x[:, c * LANES:(c + 1) * LANES]
            ms = _segsum(xc * xc, ones_bd) * (1.0 / HEAD_DIM)
            xn = xc * lax.rsqrt(ms + NORM_EPS) * gain
            swapped = jnp.where(even, pltpu.roll(xn, LANES - HEAD_DIM // 4, axis=1),
                                pltpu.roll(xn, HEAD_DIM // 4, axis=1))
            o_ref[:, c * LANES:(c + 1) * LANES] = (xn * cos + swapped * sin).astype(BF16)


def _da_prep(p, cos_t, sin_t, q_gain, k_gain, col0, width, tm):
    n = p.shape[0]
    blk0 = col0 // width
    qg = jnp.tile(q_gain, LANES // HEAD_DIM).reshape(1, LANES)
    kg = jnp.tile(k_gain, LANES // HEAD_DIM).reshape(1, LANES)
    return pl.pallas_call(
        _da_prep_kernel,
        grid=(n // tm, 3),
        in_specs=[
            pl.BlockSpec((tm, width), lambda i, s: (i, blk0 + s)),
            pl.BlockSpec((tm, LANES), lambda i, s: (i, 0)),
            pl.BlockSpec((tm, LANES), lambda i, s: (i, 0)),
            pl.BlockSpec((1, LANES), lambda i, s: (0, 0)),
            pl.BlockSpec((1, LANES), lambda i, s: (0, 0)),
        ],
        out_specs=pl.BlockSpec((None, tm, width), lambda i, s: (s, i, 0)),
        out_shape=jax.ShapeDtypeStruct((3, n, width), BF16),
        compiler_params=_cparams(("arbitrary", "arbitrary")),
        name="da_prep",
    )(p, cos_t, sin_t, qg, kg)


def _diff_attn_kernel(q_ref, k_ref, v_ref, lam_ref, sg_ref, o_ref, *, lam_init, t_ctx):
    tq = q_ref.shape[0]
    lv = lam_ref[...]
    lam = (jnp.exp(jnp.sum(lv[0:1] * lv[1:2], axis=1, keepdims=True))
           - jnp.exp(jnp.sum(lv[2:3] * lv[3:4], axis=1, keepdims=True)) + lam_init)

    def attend(keys, vals):
        q = q_ref[...]
        lane = lax.broadcasted_iota(jnp.int32, q.shape, 1)
        zero = jnp.zeros_like(q)
        outs = []
        for m in range(2):
            qm = jnp.where((lane >> 6) == m, q, zero)
            s = lax.dot_general(qm, keys, (((1,), (1,)), ((), ())), preferred_element_type=F32)
            p = jnp.exp(s - jnp.max(s, axis=1, keepdims=True))
            den = jnp.sum(p, axis=1, keepdims=True)
            outs.append(jnp.dot(p.astype(BF16), vals, preferred_element_type=F32) / den)
        o = outs[0] - lam * outs[1]
        ms = jnp.mean(o * o, axis=1, keepdims=True)
        o_ref[...] = (o * lax.rsqrt(ms + NORM_EPS) * sg_ref[...] * (1.0 - lam_init)).astype(o_ref.dtype)

    is_ctx = pl.program_id(2) < t_ctx // tq

    @pl.when(is_ctx)
    def _():
        attend(k_ref[0:t_ctx, :], v_ref[0:t_ctx, :])

    @pl.when(jnp.logical_not(is_ctx))
    def _():
        attend(k_ref[...], v_ref[...])


def _diff_attention(qkv, lam_vec, sub_gain, lam_init, batch, t_ctx, tq):
    _, n, width = qkv.shape
    heads = width // LANES
    seq = n // batch
    nq = seq // tq
    return pl.pallas_call(
        functools.partial(_diff_attn_kernel, lam_init=lam_init, t_ctx=t_ctx),
        grid=(batch, heads, nq),
        in_specs=[
            pl.BlockSpec((None, tq, LANES), lambda b, h, i: (0, b * nq + i, h)),
            pl.BlockSpec((None, seq, LANES), lambda b, h, i: (1, b, h)),
            pl.BlockSpec((None, seq, LANES), lambda b, h, i: (2, b, h)),
            pl.BlockSpec((4, HEAD_DIM), lambda b, h, i: (0, 0)),
            pl.BlockSpec((1, LANES), lambda b, h, i: (0, 0)),
        ],
        out_specs=pl.BlockSpec((tq, LANES), lambda b, h, i: (b * nq + i, h)),
        out_shape=jax.ShapeDtypeStruct((n, width), BF16),
        compiler_params=_cparams(("arbitrary", "arbitrary", "arbitrary")),
        name="diff_attn",
    )(qkv, qkv, qkv, lam_vec, sub_gain.reshape(1, LANES))


def _peer_scores_kernel(q_ref, keys_ref, o_ref):
    o_ref[...] = lax.dot_general(keys_ref[...], q_ref[...], (((1,), (1,)), ((), ())),
                                 precision=lax.Precision.HIGHEST, preferred_element_type=F32)


def _peer_scores(q, keys, tt):
    n = q.shape[0]
    hp, nk, half = keys.shape
    return pl.pallas_call(
        _peer_scores_kernel,
        grid=(n // tt, hp),
        in_specs=[
            pl.BlockSpec((tt, half), lambda i, j: (i, j)),
            pl.BlockSpec((None, nk, half), lambda i, j: (j, 0, 0)),
        ],
        out_specs=pl.BlockSpec((None, nk, tt), lambda i, j: (j, 0, i)),
        out_shape=jax.ShapeDtypeStruct((hp, nk, n), F32),
        compiler_params=_cparams(("arbitrary", "arbitrary")),
        name="peer_scores",
    )(q, keys)


SPLIT_B = 4


def _peer_select_kernel(s_ref, e1_ref, bi_ref, e2_ref, r2_ref, rank_ref, val_ref, sel_ref):
    nk, tl = s_ref.shape[1:]
    k = PEER_TOPK
    neg = -jnp.inf
    unranked = float(2 * k)

    def top(s, exact):
        key = lax.broadcasted_iota(jnp.int32, s.shape, 0)
        work = s
        rank = jnp.full(s.shape, unranked, F32)
        vals = []
        for r in range(k):
            m = jnp.max(work, axis=0, keepdims=True)
            hit = work == m
            if exact:
                hit = key == jnp.min(jnp.where(hit, key, nk), axis=0, keepdims=True)
            work = jnp.where(hit, neg, work)
            rank = jnp.where(hit, float(r), rank)
            vals.append(m)
        removed = jnp.sum(jnp.where(rank < unranked, 1.0, 0.0), axis=0, keepdims=True)
        return jnp.concatenate(vals, axis=0), rank, removed

    def stage1(exact):
        removed = jnp.zeros((1, tl), F32)
        for p in range(2):
            vals, rank, rem = top(s_ref[p], exact)
            val_ref[p] = vals
            rank_ref[p] = rank
            removed = jnp.maximum(removed, rem)
        return jnp.max(removed)

    sub = lax.broadcasted_iota(jnp.int32, (k, tl), 0)
    n_a = k // (SPLIT_B + 1)

    def stage2(exact):
        v1 = val_ref[0]
        v2 = val_ref[1]
        work, flat = [], []
        for b in range(SPLIT_B):
            ok = sub <= k // (b + 1) - 1
            work.append(jnp.where(ok, v1 + v2[b:b + 1], neg))
            flat.append(jnp.where(ok, sub * k + b, -1))
        for a in range(n_a):
            ok = (sub >= SPLIT_B) & (sub <= k // (a + 1) - 1)
            work.append(jnp.where(ok, v1[a:a + 1] + v2, neg))
            flat.append(jnp.where(ok, sub + a * k, -1))
        count = [jnp.zeros((k, tl), F32) for _ in work]
        top_sum = v1[0:1] + v2[0:1]
        z = jnp.zeros((1, tl), F32)
        for _ in range(k):
            m = jnp.max(functools.reduce(jnp.maximum, work), axis=0, keepdims=True)
            z = z + jnp.exp(m - top_sum)
            if exact:
                cand = functools.reduce(jnp.minimum, [jnp.where(w == m, f, k * k) for w, f in zip(work, flat)])
                idx = jnp.min(cand, axis=0, keepdims=True)
            for n in range(len(work)):
                hit = (flat[n] == idx) if exact else (work[n] == m)
                work[n] = jnp.where(hit, neg, work[n])
                count[n] = count[n] + jnp.where(hit, 1.0, 0.0)
        per_a = functools.reduce(jnp.add, count[:SPLIT_B])
        for a in range(n_a):
            extra = jnp.sum(count[SPLIT_B + a], axis=0, keepdims=True)
            per_a = per_a + jnp.where(sub == a, extra, 0.0)
        sel_ref[0:k, :] = per_a
        sel_ref[k:k + 1, :] = z
        return jnp.max(jnp.sum(per_a, axis=0, keepdims=True))

    @pl.when(stage1(False) > float(k))
    def _():
        stage1(True)

    @pl.when(stage2(False) > float(k))
    def _():
        stage2(True)

    s1 = s_ref[0]
    rank1 = rank_ref[0]
    bi = jnp.zeros(s1.shape, F32)
    for a in range(k):
        bi = jnp.where(rank1 == float(a), sel_ref[a:a + 1, :], bi)
    bi_ref[...] = bi
    e1_ref[...] = jnp.exp(s1 - val_ref[0, 0:1, :]) / sel_ref[k:k + 1, :]
    e2_ref[...] = jnp.exp(s_ref[1] - val_ref[1, 0:1, :]).astype(e2_ref.dtype)
    r2_ref[...] = rank_ref[1].astype(r2_ref.dtype)


def _peer_select(scores, heads, tl):
    hp, nk, n = scores.shape
    k = PEER_TOPK
    out = jax.ShapeDtypeStruct((heads, nk, n), F32)
    spec = pl.BlockSpec((None, nk, tl), lambda t, h: (h, 0, t))
    return pl.pallas_call(
        _peer_select_kernel,
        grid=(n // tl, heads),
        in_specs=[pl.BlockSpec((None, 2, nk, tl), lambda t, h: (h, 0, 0, t))],
        out_specs=[spec] * 4,
        out_shape=[out] * 4,
        scratch_shapes=[pltpu.VMEM((2, nk, tl), F32), pltpu.VMEM((2, k, tl), F32),
                        pltpu.VMEM((k + 8, tl), F32)],
        compiler_params=_cparams(("arbitrary", "arbitrary")),
        name="peer_select",
    )(scores.reshape(heads, 2, nk, n))


def _peer_dense_kernel(h_ref, u_ref, v_ref, e1_ref, bi_ref, e2_ref, r2_ref, o_ref, *, nk, chunk):
    e = pl.program_id(1)
    heads = e1_ref.shape[0]
    et = u_ref.shape[0]

    @pl.when(e == 0)
    def _():
        o_ref[...] = jnp.zeros_like(o_ref)

    total = None
    for c in range(et // chunk):
        a = lax.dot_general(u_ref[c * chunk:(c + 1) * chunk, :], h_ref[...], (((1,), (1,)), ((), ())),
                            preferred_element_type=F32)
        act = 0.5 * a * (1.0 + lax.erf(a * (2.0 ** -0.5)))
        parts = []
        for ii in range(chunk // nk):
            i = (e * et + c * chunk) // nk + ii
            g = jnp.zeros((nk, a.shape[1]), F32)
            for hh in range(heads):
                e1 = e1_ref[hh, pl.ds(i, 1), :]
                bi = bi_ref[hh, pl.ds(i, 1), :]
                g = g + jnp.where(r2_ref[hh] < bi, e2_ref[hh] * e1, 0.0)
            parts.append(g * act[ii * nk:(ii + 1) * nk])
        m = jnp.concatenate(parts, axis=0) if len(parts) > 1 else parts[0]
        contrib = jnp.dot(m.T.astype(BF16), v_ref[c * chunk:(c + 1) * chunk, :], preferred_element_type=F32)
        total = contrib if total is None else total + contrib
    o_ref[...] += total


def _peer_dense(h, u, v, e1, bi, e2, r2, tl, et, chunk):
    n, d = h.shape
    ne = u.shape[0]
    heads, nk, _ = e1.shape
    fac = pl.BlockSpec((heads, nk, tl), lambda t, e: (0, 0, t))
    return pl.pallas_call(
        functools.partial(_peer_dense_kernel, nk=nk, chunk=chunk),
        grid=(n // tl, ne // et),
        in_specs=[
            pl.BlockSpec((tl, d), lambda t, e: (t, 0)),
            pl.BlockSpec((et, d), lambda t, e: (e, 0)),
            pl.BlockSpec((et, d), lambda t, e: (e, 0)),
            fac, fac, fac, fac,
        ],
        out_specs=pl.BlockSpec((tl, d), lambda t, e: (t, 0)),
        out_shape=jax.ShapeDtypeStruct((n, d), F32),
        compiler_params=_cparams(("arbitrary", "arbitrary")),
        name="peer_dense",
    )(h, u, v, e1, bi, e2, r2)


def _gated_add_kernel(gid_ref, x_ref, y_ref, gate_ref, o_ref):
    del gid_ref
    o_ref[...] = x_ref[...] + gate_ref[...] * y_ref[...]


def _gated_add(x, y, gid, gate, tm):
    n, d = x.shape
    spec = pl.BlockSpec((tm, d), lambda i, g: (i, 0))
    return pl.pallas_call(
        _gated_add_kernel,
        grid_spec=pltpu.PrefetchScalarGridSpec(
            num_scalar_prefetch=1,
            grid=(n // tm,),
            in_specs=[spec, spec, pl.BlockSpec((None, 1, d), lambda i, g: (g[i], 0, 0))],
            out_specs=spec,
        ),
        out_shape=jax.ShapeDtypeStruct((n, d), F32),
        compiler_params=_cparams(("arbitrary",)),
        name="gated_add",
    )(gid, x, y, gate)


def _rope_tables(t_lat, t_ctx, batch):
    rows = t_lat // GRID_W
    row = jnp.repeat(jnp.arange(rows, dtype=jnp.int32), GRID_W)
    col = jnp.tile(jnp.arange(GRID_W, dtype=jnp.int32), rows)
    nfreq = HEAD_DIM // 4
    inv_freq = ROPE_THETA ** (-jnp.arange(nfreq, dtype=F32) / nfreq)
    ang_r = row.astype(F32)[:, None] * inv_freq
    ang_c = col.astype(F32)[:, None] * inv_freq
    cos = jnp.concatenate([jnp.cos(ang_r)] * 2 + [jnp.cos(ang_c)] * 2, axis=1)
    sin = jnp.concatenate([-jnp.sin(ang_r), jnp.sin(ang_r), -jnp.sin(ang_c), jnp.sin(ang_c)], axis=1)
    reps = LANES // HEAD_DIM
    cos = jnp.concatenate([jnp.ones((t_ctx, LANES), F32), jnp.tile(cos, (1, reps))], axis=0)
    sin = jnp.concatenate([jnp.zeros((t_ctx, LANES), F32), jnp.tile(sin, (1, reps))], axis=0)
    return jnp.tile(cos, (batch, 1)), jnp.tile(sin, (batch, 1))


def kernel(x, c, ctx, c_ctx, ada_w, ada_b, norm_gain, w_in, conv_w, rw_w0, rw_w_up, rw_a0, rw_a_up,
           rw_g_up, rw_k_k, rw_k_a, rw_r_k, rw_gn_gain, rw_gn_bias, da_q_gain, da_k_gain, da_lambda,
           da_sub_gain, w_out, peer_wq, peer_keys, peer_u, peer_v):
    batch, t_lat, d = x.shape
    t_ctx = ctx.shape[1]
    seq = t_ctx + t_lat
    n = batch * seq
    depth = ada_w.shape[0]
    rw = rw_k_k.shape[1]
    lora_w = rw_w_up.shape[2] + rw_a_up.shape[2] + rw_g_up.shape[1]
    da_w = w_out.shape[1] - rw
    p_heads, _, nk, half = peer_keys.shape[1:]
    assert 2 * batch * (rw // HEAD_DIM) == LANES, "the recurrence kernel fills the lanes with (half, batch, head)"

    tm = _tile(math.gcd(t_lat, t_ctx), 256)
    tile_pos = (jnp.arange(n // tm, dtype=jnp.int32) * tm) % seq
    gid = jnp.where(tile_pos < t_ctx, batch, jnp.arange(n // tm, dtype=jnp.int32) * tm // seq)
    first = ((tile_pos == 0) | (tile_pos == t_ctx)).astype(jnp.int32)
    last = ((tile_pos + tm == t_ctx) | (tile_pos + tm == seq)).astype(jnp.int32)

    xs = jnp.concatenate([ctx, x], axis=1).reshape(n, d)
    c8 = jnp.concatenate([c, c_ctx[None], jnp.zeros((8 - batch - 1, d), F32)], axis=0)
    mods = _modulation(c8, ada_w, ada_b)[:, :batch + 1].reshape(depth, batch + 1, 6, 1, d)
    cos_t, sin_t = _rope_tables(t_lat, t_ctx, batch)

    cols = jnp.concatenate([jnp.arange(3 * rw), jnp.arange(3 * rw + lora_w, w_in.shape[2]),
                            jnp.arange(3 * rw, 3 * rw + lora_w)])
    tn_in = _tile(w_in.shape[2], 1280, LANES)
    ts = _tile(tm, 128)

    for l in range(depth):
        lam_init = 0.8 - 0.6 * math.exp(-0.3 * l)
        mod = [mods[l, :, j] for j in range(6)]

        p = _norm_proj_wide(xs, gid, norm_gain[l, 0], mod[0], mod[1], w_in[l][:, cols].astype(BF16), tm,
                            _tile(n, 1024), tn_in)
        q8, v, bonus, gate = _rwkv_prep(
            p, first, last, conv_w[l], rw_w0[l], rw_w_up[l], rw_a0[l], rw_a_up[l], rw_g_up[l],
            rw_k_k[l], rw_k_a[l], rw_r_k[l].reshape(-1), tm)
        qs, vs = _to_streams(q8, v, batch, ts)
        yf, yb = _rwkv_scan(qs, vs, t_ctx, tc=_tile(math.gcd(t_lat, t_ctx), 32))
        rw_out = _rwkv_finish(yf, yb, bonus, gate, rw_gn_gain[l], rw_gn_bias[l], batch, ts)

        qkv = _da_prep(p, cos_t, sin_t, da_q_gain[l], da_k_gain[l], 3 * rw, da_w, tm)
        da_out = _diff_attention(qkv, da_lambda[l], da_sub_gain[l], lam_init, batch, t_ctx, tm)
        xs = _gated_proj(rw_out, da_out, w_out[l].astype(BF16), xs, gid, mod[2], tm)

        q, h = _norm_proj(xs, gid, norm_gain[l, 1], mod[3], mod[4], peer_wq[l].astype(BF16), tm,
                          _tile(peer_wq.shape[2], 2048, LANES), emit_h=True)
        scores = _peer_scores(q, peer_keys[l].reshape(2 * p_heads, nk, half), _tile(n, 1024, LANES))
        e1, bi, e2, r2 = _peer_select(scores, p_heads, _tile(n, 512, LANES))
        out = _peer_dense(h, peer_u[l].astype(BF16), peer_v[l].astype(BF16), e1, bi, e2, r2,
                          tl=_tile(n, 512, LANES), et=4 * nk, chunk=2 * nk)
        xs = _gated_add(xs, out, gid, mod[5], tm)

    return xs.reshape(batch, seq, d)[:, t_ctx:]
```

```python
import functools
import math

import jax
import jax.numpy as jnp
from jax import lax
from jax.experimental import pallas as pl
from jax.experimental.pallas import tpu as pltpu

F32 = jnp.float32
BF16 = jnp.bfloat16

HEAD_DIM = 64
LANES = 128
GRID_W = 64
ROPE_THETA = 10000.0
NORM_EPS = 1e-6
GN_EPS = 64e-5
L2_EPS = 1e-12
PEER_TOPK = 16
LOG2_E = 1.4426950408889634
VMEM_LIMIT = 56 * 1024 * 1024


def _cparams(sem):
    return pltpu.CompilerParams(dimension_semantics=sem, vmem_limit_bytes=VMEM_LIMIT)


def _mm(a, b):
    return jnp.dot(a.astype(BF16), b.astype(BF16), preferred_element_type=F32)


def _segment_ones():
    r = lax.broadcasted_iota(jnp.int32, (LANES, LANES), 0) >> 6
    c = lax.broadcasted_iota(jnp.int32, (LANES, LANES), 1) >> 6
    return jnp.where(r == c, 1.0, 0.0).astype(BF16)


def _segsum(x, ones_bd):
    hi = x.astype(BF16)
    lo = (x - hi.astype(F32)).astype(BF16)
    return (jnp.dot(hi, ones_bd, preferred_element_type=F32)
            + jnp.dot(lo, ones_bd, preferred_element_type=F32))


def _tile(n, pref, align=8):
    for t in range(min(n, pref) // align * align, 0, -align):
        if n % t == 0:
            return t
    return n


def _mod_kernel(c_ref, w_ref, b_ref, o_ref):
    cv = c_ref[...]
    act = cv * jax.nn.sigmoid(cv)
    o_ref[...] = _mm(act, w_ref[...]) + b_ref[...]


def _modulation(c8, ada_w, ada_b):
    depth, d, n6 = ada_w.shape
    tn = _tile(n6, 1536)
    return pl.pallas_call(
        _mod_kernel,
        grid=(depth, n6 // tn),
        in_specs=[
            pl.BlockSpec((8, d), lambda l, j: (0, 0)),
            pl.BlockSpec((None, d, tn), lambda l, j: (l, 0, j)),
            pl.BlockSpec((None, 1, tn), lambda l, j: (l, 0, j)),
        ],
        out_specs=pl.BlockSpec((None, 8, tn), lambda l, j: (l, 0, j)),
        out_shape=jax.ShapeDtypeStruct((depth, 8, n6), F32),
        compiler_params=_cparams(("arbitrary", "arbitrary")),
        name="modulation",
    )(c8, ada_w, ada_b.reshape(depth, 1, n6))


def _norm_proj_kernel(gid_ref, x_ref, gain_ref, shift_ref, scale_ref, w_ref, o_ref, *rest, emit_h):
    del gid_ref
    hs_ref = rest[-1]

    @pl.when(pl.program_id(1) == 0)
    def _():
        x = x_ref[...]
        ms = jnp.mean(x * x, axis=-1, keepdims=True)
        y = x * lax.rsqrt(ms + NORM_EPS) * gain_ref[...]
        h = y * (1.0 + scale_ref[...]) + shift_ref[...]
        hs_ref[...] = h.astype(BF16)
        if emit_h:
            rest[0][...] = h.astype(BF16)

    o_ref[...] = jnp.dot(hs_ref[...], w_ref[...], preferred_element_type=F32)


def _norm_proj(x, gid, gain, shift, scale, w, tm, tn, emit_h=False):
    n, d = x.shape
    nout = w.shape[1]
    out_shape = [jax.ShapeDtypeStruct((n, nout), F32)]
    out_specs = [pl.BlockSpec((tm, tn), lambda i, j, g: (i, j))]
    if emit_h:
        out_shape.append(jax.ShapeDtypeStruct((n, d), BF16))
        out_specs.append(pl.BlockSpec((tm, d), lambda i, j, g: (i, 0)))
    res = pl.pallas_call(
        functools.partial(_norm_proj_kernel, emit_h=emit_h),
        grid_spec=pltpu.PrefetchScalarGridSpec(
            num_scalar_prefetch=1,
            grid=(n // tm, nout // tn),
            in_specs=[
                pl.BlockSpec((tm, d), lambda i, j, g: (i, 0)),
                pl.BlockSpec((1, d), lambda i, j, g: (0, 0)),
                pl.BlockSpec((None, 1, d), lambda i, j, g: (g[i], 0, 0)),
                pl.BlockSpec((None, 1, d), lambda i, j, g: (g[i], 0, 0)),
                pl.BlockSpec((d, tn), lambda i, j, g: (0, j)),
            ],
            out_specs=out_specs,
            scratch_shapes=[pltpu.VMEM((tm, d), BF16)],
        ),
        out_shape=out_shape,
        compiler_params=_cparams(("arbitrary", "arbitrary")),
        name="norm_proj",
    )(gid, x, gain.reshape(1, d), shift, scale, w)
    return res if emit_h else res[0]


def _norm_mod_kernel(gid_ref, x_ref, gain_ref, shift_ref, scale_ref, o_ref):
    del gid_ref
    x = x_ref[...]
    ms = jnp.mean(x * x, axis=-1, keepdims=True)
    y = x * lax.rsqrt(ms + NORM_EPS) * gain_ref[...]
    o_ref[...] = (y * (1.0 + scale_ref[...]) + shift_ref[...]).astype(BF16)


def _matmul_kernel(h_ref, w_ref, o_ref):
    o_ref[...] = jnp.dot(h_ref[...], w_ref[...], preferred_element_type=F32)


def _norm_proj_wide(x, gid, gain, shift, scale, w, tm, tm_mm, tn):
    n, d = x.shape
    nout = w.shape[1]
    h = pl.pallas_call(
        _norm_mod_kernel,
        grid_spec=pltpu.PrefetchScalarGridSpec(
            num_scalar_prefetch=1,
            grid=(n // tm,),
            in_specs=[
                pl.BlockSpec((tm, d), lambda i, g: (i, 0)),
                pl.BlockSpec((1, d), lambda i, g: (0, 0)),
                pl.BlockSpec((None, 1, d), lambda i, g: (g[i], 0, 0)),
                pl.BlockSpec((None, 1, d), lambda i, g: (g[i], 0, 0)),
            ],
            out_specs=pl.BlockSpec((tm, d), lambda i, g: (i, 0)),
        ),
        out_shape=jax.ShapeDtypeStruct((n, d), BF16),
        compiler_params=_cparams(("arbitrary",)),
        name="norm_mod",
    )(gid, x, gain.reshape(1, d), shift, scale)
    return pl.pallas_call(
        _matmul_kernel,
        grid=(nout // tn, n // tm_mm),
        in_specs=[
            pl.BlockSpec((tm_mm, d), lambda j, i: (i, 0)),
            pl.BlockSpec((d, tn), lambda j, i: (0, j)),
        ],
        out_specs=pl.BlockSpec((tm_mm, tn), lambda j, i: (i, j)),
        out_shape=jax.ShapeDtypeStruct((n, nout), F32),
        compiler_params=_cparams(("arbitrary", "arbitrary")),
        name="proj_matmul",
    )(h, w)


def _gated_proj_kernel(gid_ref, y1_ref, y2_ref, w1_ref, w2_ref, x_ref, gate_ref, o_ref):
    del gid_ref
    mix = (jnp.dot(y1_ref[...], w1_ref[...], preferred_element_type=F32)
           + jnp.dot(y2_ref[...], w2_ref[...], preferred_element_type=F32))
    o_ref[...] = x_ref[...] + gate_ref[...] * mix


def _gated_proj(y1, y2, w, x, gid, gate, tm):
    n, k1 = y1.shape
    k2 = y2.shape[1]
    d = w.shape[1]
    return pl.pallas_call(
        _gated_proj_kernel,
        grid_spec=pltpu.PrefetchScalarGridSpec(
            num_scalar_prefetch=1,
            grid=(n // tm,),
            in_specs=[
                pl.BlockSpec((tm, k1), lambda i, g: (i, 0)),
                pl.BlockSpec((tm, k2), lambda i, g: (i, 0)),
                pl.BlockSpec((k1, d), lambda i, g: (0, 0)),
                pl.BlockSpec((k2, d), lambda i, g: (k1 // k2, 0)),
                pl.BlockSpec((tm, d), lambda i, g: (i, 0)),
                pl.BlockSpec((None, 1, d), lambda i, g: (g[i], 0, 0)),
            ],
            out_specs=pl.BlockSpec((tm, d), lambda i, g: (i, 0)),
        ),
        out_shape=jax.ShapeDtypeStruct((n, d), F32),
        compiler_params=_cparams(("arbitrary",)),
        name="gated_proj",
    )(gid, y1, y2, w, w, x, gate)


def _rwkv_prep_kernel(first_ref, last_ref, x_ref, prev_ref, next_ref, tail_ref, cw_ref,
                      w0_ref, wup_ref, a0_ref, aup_ref, gup_ref, kk_ref, ka_ref, rk_ref,
                      q_out, v_out, bonus_out, gate_out, *, rw, lw, la):
    i = pl.program_id(0)
    tm = x_ref.shape[0]
    x = x_ref[...]
    row = lax.broadcasted_iota(jnp.int32, x.shape, 0)
    keep_prev = jnp.where(first_ref[i] == 1, 0.0, 1.0)
    keep_next = jnp.where(last_ref[i] == 1, 0.0, 1.0)
    halo_prev = prev_ref[7:8, :] * keep_prev
    halo_next = next_ref[0:1, :] * keep_next
    x_prev = jnp.where(row == 0, halo_prev, pltpu.roll(x, 1, axis=0))
    x_next = jnp.where(row == tm - 1, halo_next, pltpu.roll(x, tm - 1, axis=0))
    conv = x_prev * cw_ref[0:1, :] + x * cw_ref[1:2, :] + x_next * cw_ref[2:3, :]
    r = conv[:, 0:rw]
    k = conv[:, rw:2 * rw]
    v = conv[:, 2 * rw:3 * rw]
    tail = tail_ref[...]
    w_lo = jnp.tanh(tail[:, 0:lw])
    a_lo = tail[:, lw:lw + la]
    g_lo = jax.nn.sigmoid(tail[:, lw + la:])
    ones_bd = _segment_ones()

    def seg(t):
        return jnp.concatenate(
            [_segsum(t[:, c * LANES:(c + 1) * LANES], ones_bd) for c in range(rw // LANES)], axis=1)

    kap = k * kk_ref[...]
    nrm = jnp.sqrt(seg(kap * kap))
    kap = kap / jnp.maximum(nrm, L2_EPS)
    q_out[0] = r
    q_out[1] = kap
    v_out[...] = v
    gate_out[...] = _mm(g_lo, gup_ref[...])
    bonus = jnp.zeros_like(r)
    for d in range(2):
        z = w0_ref[d:d + 1, :] + _mm(w_lo, wup_ref[d])
        log_w = -jax.nn.softplus(-z) - 0.5
        q_out[2 + d] = jnp.exp(-jnp.exp(log_w))
        a = jax.nn.sigmoid(a0_ref[d:d + 1, :] + _mm(a_lo, aup_ref[d]))
        k_mod = k * (1.0 + (a - 1.0) * ka_ref[...])
        q_out[4 + d] = k_mod
        q_out[6 + d] = kap * a
        bonus = bonus + seg(r * k_mod * rk_ref[...]) * v
    bonus_out[...] = bonus


def _rwkv_prep(p, first, last, conv_w, w0, w_up, a0, a_up, g_up, k_k, k_a, r_k, tm):
    n = p.shape[0]
    rw = k_k.shape[0]
    lw, la, lg = w_up.shape[1], a_up.shape[1], g_up.shape[0]
    tail_w = lw + la + lg
    tail_blk = (p.shape[1] - tail_w) // tail_w
    nb8 = n // 8
    row2 = lambda i, f, l: (0, 0)
    row3 = lambda i, f, l: (0, 0, 0)
    one = jax.ShapeDtypeStruct((n, rw), F32)
    one_spec = pl.BlockSpec((tm, rw), lambda i, f, l: (i, 0))
    return pl.pallas_call(
        functools.partial(_rwkv_prep_kernel, rw=rw, lw=lw, la=la),
        grid_spec=pltpu.PrefetchScalarGridSpec(
            num_scalar_prefetch=2,
            grid=(n // tm,),
            in_specs=[
                pl.BlockSpec((tm, 3 * rw), lambda i, f, l: (i, 0)),
                pl.BlockSpec((8, 3 * rw), lambda i, f, l: (jnp.maximum(i * (tm // 8) - 1, 0), 0)),
                pl.BlockSpec((8, 3 * rw), lambda i, f, l: (jnp.minimum((i + 1) * (tm // 8), nb8 - 1), 0)),
                pl.BlockSpec((tm, tail_w), lambda i, f, l: (i, tail_blk)),
                pl.BlockSpec((3, 3 * rw), row2),
                pl.BlockSpec((2, rw), row2),
                pl.BlockSpec((2, lw, rw), row3),
                pl.BlockSpec((2, rw), row2),
                pl.BlockSpec((2, la, rw), row3),
                pl.BlockSpec((lg, rw), row2),
                pl.BlockSpec((1, rw), row2),
                pl.BlockSpec((1, rw), row2),
                pl.BlockSpec((1, rw), row2),
            ],
            out_specs=[pl.BlockSpec((8, tm, rw), lambda i, f, l: (0, i, 0)), one_spec, one_spec, one_spec],
        ),
        out_shape=[jax.ShapeDtypeStruct((8, n, rw), F32), one, one, one],
        compiler_params=_cparams(("arbitrary",)),
        name="rwkv_prep",
    )(first, last, p, p, p, p, conv_w, w0, w_up, a0, a_up, g_up,
      k_k.reshape(1, rw), k_a.reshape(1, rw), r_k.reshape(1, rw))


def _to_streams_kernel(x_ref, o_ref, xt_ref, *, split_value):
    batch, tm, rw = x_ref.shape
    heads = rw // HEAD_DIM
    for b in range(batch):
        xt_ref[b] = x_ref[b].T
    rows = HEAD_DIM // 2 if split_value else HEAD_DIM
    for j in range(rows):
        second = j + HEAD_DIM // 2 if split_value else j
        parts = [xt_ref[b, pl.ds(j, heads, stride=HEAD_DIM), :] for b in range(batch)]
        parts += [xt_ref[b, pl.ds(second, heads, stride=HEAD_DIM), :] for b in range(batch)]
        slab = jnp.concatenate(parts, axis=0).T
        if split_value:
            o_ref[:, j, :] = slab
        else:
            o_ref[j] = slab


def _to_streams(q, v, batch, tm):
    nq, n, rw = q.shape
    seq = n // batch
    scratch = [pltpu.VMEM((batch, rw, tm), F32)]
    qs = pl.pallas_call(
        functools.partial(_to_streams_kernel, split_value=False),
        grid=(nq, seq // tm),
        in_specs=[pl.BlockSpec((None, batch, tm, rw), lambda a, i: (a, 0, i, 0))],
        out_specs=pl.BlockSpec((None, HEAD_DIM, tm, LANES), lambda a, i: (a, 0, i, 0)),
        out_shape=jax.ShapeDtypeStruct((nq, HEAD_DIM, seq, LANES), F32),
        scratch_shapes=scratch,
        compiler_params=_cparams(("arbitrary", "arbitrary")),
        name="to_streams",
    )(q.reshape(nq, batch, seq, rw))
    vs = pl.pallas_call(
        functools.partial(_to_streams_kernel, split_value=True),
        grid=(seq // tm,),
        in_specs=[pl.BlockSpec((batch, tm, rw), lambda i: (0, i, 0))],
        out_specs=pl.BlockSpec((tm, HEAD_DIM // 2, LANES), lambda i: (i, 0, 0)),
        out_shape=jax.ShapeDtypeStruct((seq, HEAD_DIM // 2, LANES), F32),
        scratch_shapes=scratch,
        compiler_params=_cparams(("arbitrary",)),
        name="to_streams_v",
    )(v.reshape(batch, seq, rw))
    return qs, vs


def _rwkv_scan_kernel(rf, kapf, wf, kf, bf, vf, rb, kapb, wb, kb, bb, vb, yf, yb, s_ref):
    hd, tc, ns = rf.shape
    half = vf.shape[1]

    @pl.when(pl.program_id(0) == 0)
    def _():
        s_ref[...] = jnp.zeros_like(s_ref)

    dirs = ((rf, kapf, wf, kf, bf, vf, yf), (rb, kapb, wb, kb, bb, vb, yb))

    def state_times_kappa(d, t):
        acc = jnp.zeros((half, ns), F32)
        for kk in range(hd):
            acc = acc + s_ref[d, kk] * dirs[d][1][kk, pl.ds(t, 1), :]
        return acc

    def update(d, t, t_next, sa):
        r_ref, kap_ref, w_ref, k_ref, b_ref, v_ref, y_ref = dirs[d]
        vv = v_ref[t]
        y = jnp.zeros((half, ns), F32)
        sa_next = jnp.zeros((half, ns), F32)
        for kk in range(hd):
            s_new = (s_ref[d, kk] * w_ref[kk, pl.ds(t, 1), :]
                     + (vv * k_ref[kk, pl.ds(t, 1), :] - sa * b_ref[kk, pl.ds(t, 1), :]))
            s_ref[d, kk] = s_new
            y = y + s_new * r_ref[kk, pl.ds(t, 1), :]
            sa_next = sa_next + s_new * kap_ref[kk, pl.ds(t_next, 1), :]
        y_ref[t] = y
        return sa_next

    def step(j, carry):
        sa_f, sa_b = carry
        sa_f = update(0, j, jnp.minimum(j + 1, tc - 1), sa_f)
        tb = tc - 1 - j
        sa_b = update(1, tb, jnp.maximum(tb - 1, 0), sa_b)
        return sa_f, sa_b

    lax.fori_loop(0, tc, step, (state_times_kappa(0, 0), state_times_kappa(1, tc - 1)))


def _rwkv_scan(qs, vs, t_ctx, tc):
    _, hd, seq, ns = qs.shape
    half = vs.shape[1]
    nb = seq // tc
    nbc = t_ctx // tc

    def back_block(i):
        return jnp.where(i < nbc, nbc - 1 - i, nb - 1 - (i - nbc))

    def qspec(a, back):
        if back:
            return pl.BlockSpec((None, hd, tc, ns), lambda i: (a, 0, back_block(i), 0))
        return pl.BlockSpec((None, hd, tc, ns), lambda i: (a, 0, i, 0))

    def vspec(back):
        if back:
            return pl.BlockSpec((tc, half, ns), lambda i: (back_block(i), 0, 0))
        return pl.BlockSpec((tc, half, ns), lambda i: (i, 0, 0))

    in_specs = ([qspec(a, False) for a in (0, 1, 2, 4, 6)] + [vspec(False)]
                + [qspec(a, True) for a in (0, 1, 3, 5, 7)] + [vspec(True)])
    out = jax.ShapeDtypeStruct((seq, half, ns), F32)
    return pl.pallas_call(
        _rwkv_scan_kernel,
        grid=(nb,),
        in_specs=in_specs,
        out_specs=[vspec(False), vspec(True)],
        out_shape=[out, out],
        scratch_shapes=[pltpu.VMEM((2, hd, half, ns), F32)],
        compiler_params=_cparams(("arbitrary",)),
        name="rwkv_scan",
    )(*([qs] * 5 + [vs] + [qs] * 5 + [vs]))


def _rwkv_finish_kernel(yf_ref, yb_ref, bonus_ref, gate_ref, gain_ref, bias_ref, o_ref, xt_ref):
    batch, tm, rw = bonus_ref.shape
    heads = rw // HEAD_DIM
    half = yf_ref.shape[1]
    for j in range(half):
        slab = (yf_ref[:, j, :] + yb_ref[:, j, :]).T
        for g in range(2 * batch):
            xt_ref[g % batch, pl.ds(j + half * (g // batch), heads, stride=HEAD_DIM), :] = (
                slab[g * heads:(g + 1) * heads])
    ones_bd = _segment_ones()
    for b in range(batch):
        y = xt_ref[b].T
        outs = []
        for c in range(rw // LANES):
            yc = y[:, c * LANES:(c + 1) * LANES]
            mu = _segsum(yc, ones_bd) * (1.0 / HEAD_DIM)
            dev = yc - mu
            var = _segsum(dev * dev, ones_bd) * (1.0 / HEAD_DIM)
            outs.append(dev * lax.rsqrt(var + GN_EPS))
        yn = jnp.concatenate(outs, axis=1)
        o_ref[b] = ((yn * gain_ref[...] + bias_ref[...] + bonus_ref[b]) * gate_ref[b]).astype(o_ref.dtype)


def _rwkv_finish(yf, yb, bonus, gate, gain, bias, batch, tm):
    seq, half, ns = yf.shape
    n, rw = bonus.shape
    yspec = pl.BlockSpec((tm, half, ns), lambda i: (i, 0, 0))
    spec = pl.BlockSpec((batch, tm, rw), lambda i: (0, i, 0))
    vec = pl.BlockSpec((1, rw), lambda i: (0, 0))
    out = pl.pallas_call(
        _rwkv_finish_kernel,
        grid=(seq // tm,),
        in_specs=[yspec, yspec, spec, spec, vec, vec],
        out_specs=spec,
        out_shape=jax.ShapeDtypeStruct((batch, seq, rw), BF16),
        scratch_shapes=[pltpu.VMEM((batch, rw, tm), F32)],
        compiler_params=_cparams(("arbitrary",)),
        name="rwkv_finish",
    )(yf, yb, bonus.reshape(batch, seq, rw), gate.reshape(batch, seq, rw),
      gain.reshape(1, rw), bias.reshape(1, rw))
    return out.reshape(n, rw)


def _da_prep_kernel(x_ref, cos_ref, sin_ref, qg_ref, kg_ref, o_ref):
    s = pl.program_id(1)

    @pl.when(s == 2)
    def _():
        o_ref[...] = x_ref[...].astype(BF16)

    @pl.when(s < 2)
    def _():
        ones_bd = _segment_ones()
        gain = jnp.where(s == 0, qg_ref[...] * (HEAD_DIM ** -0.5 * LOG2_E), kg_ref[...])
        cos = cos_ref[...]
        sin = sin_ref[...]
        lane = lax.broadcasted_iota(jnp.int32, cos.shape, 1)
        even = ((lane >> 4) & 1) == 0
        x = x_ref[...]
        for c in range(x.shape[1] // LANES):
            xc = x[:, c * LANES:(c + 1) * LANES]
            ms = _segsum(xc * xc, ones_bd) * (1.0 / HEAD_DIM)
            xn = xc * lax.rsqrt(ms + NORM_EPS) * gain
            swapped = jnp.where(even, pltpu.roll(xn, LANES - HEAD_DIM // 4, axis=1),
                                pltpu.roll(xn, HEAD_DIM // 4, axis=1))
            o_ref[:, c * LANES:(c + 1) * LANES] = (xn * cos + swapped * sin).astype(BF16)


def _da_prep(p, cos_t, sin_t, q_gain, k_gain, col0, width, tm):
    n = p.shape[0]
    blk0 = col0 // width
    qg = jnp.tile(q_gain, LANES // HEAD_DIM).reshape(1, LANES)
    kg = jnp.tile(k_gain, LANES // HEAD_DIM).reshape(1, LANES)
    return pl.pallas_call(
        _da_prep_kernel,
        grid=(n // tm, 3),
        in_specs=[
            pl.BlockSpec((tm, width), lambda i, s: (i, blk0 + s)),
            pl.BlockSpec((tm, LANES), lambda i, s: (i, 0)),
            pl.BlockSpec((tm, LANES), lambda i, s: (i, 0)),
            pl.BlockSpec((1, LANES), lambda i, s: (0, 0)),
            pl.BlockSpec((1, LANES), lambda i, s: (0, 0)),
        ],
        out_specs=pl.BlockSpec((None, tm, width), lambda i, s: (s, i, 0)),
        out_shape=jax.ShapeDtypeStruct((3, n, width), BF16),
        compiler_params=_cparams(("arbitrary", "arbitrary")),
        name="da_prep",
    )(p, cos_t, sin_t, qg, kg)


def _diff_attn_kernel(q_ref, k_ref, v_ref, lam_ref, sg_ref, o_ref, *, lam_init, t_ctx):
    tq = q_ref.shape[0]
    lv = lam_ref[...]
    lam = (jnp.exp(jnp.sum(lv[0:1] * lv[1:2], axis=1, keepdims=True))
           - jnp.exp(jnp.sum(lv[2:3] * lv[3:4], axis=1, keepdims=True)) + lam_init)

    def attend(keys, vals):
        q = q_ref[...]
        lane = lax.broadcasted_iota(jnp.int32, q.shape, 1)
        zero = jnp.zeros_like(q)
        outs = []
        for m in range(2):
            qm = jnp.where((lane >> 6) == m, q, zero)
            s = lax.dot_general(qm, keys, (((1,), (1,)), ((), ())), preferred_element_type=F32)
            p = jnp.exp2(s - jnp.max(s, axis=1, keepdims=True))
            den = jnp.sum(p, axis=1, keepdims=True)
            outs.append(jnp.dot(p.astype(BF16), vals, preferred_element_type=F32) / den)
        o = outs[0] - lam * outs[1]
        ms = jnp.mean(o * o, axis=1, keepdims=True)
        o_ref[...] = (o * lax.rsqrt(ms + NORM_EPS) * sg_ref[...] * (1.0 - lam_init)).astype(o_ref.dtype)

    is_ctx = pl.program_id(2) < t_ctx // tq

    @pl.when(is_ctx)
    def _():
        attend(k_ref[0:t_ctx, :], v_ref[0:t_ctx, :])

    @pl.when(jnp.logical_not(is_ctx))
    def _():
        attend(k_ref[...], v_ref[...])


def _diff_attention(qkv, lam_vec, sub_gain, lam_init, batch, t_ctx, tq):
    _, n, width = qkv.shape
    heads = width // LANES
    seq = n // batch
    nq = seq // tq
    return pl.pallas_call(
        functools.partial(_diff_attn_kernel, lam_init=lam_init, t_ctx=t_ctx),
        grid=(batch, heads, nq),
        in_specs=[
            pl.BlockSpec((None, tq, LANES), lambda b, h, i: (0, b * nq + i, h)),
            pl.BlockSpec((None, seq, LANES), lambda b, h, i: (1, b, h)),
            pl.BlockSpec((None, seq, LANES), lambda b, h, i: (2, b, h)),
            pl.BlockSpec((4, HEAD_DIM), lambda b, h, i: (0, 0)),
            pl.BlockSpec((1, LANES), lambda b, h, i: (0, 0)),
        ],
        out_specs=pl.BlockSpec((tq, LANES), lambda b, h, i: (b * nq + i, h)),
        out_shape=jax.ShapeDtypeStruct((n, width), BF16),
        compiler_params=_cparams(("arbitrary", "arbitrary", "arbitrary")),
        name="diff_attn",
    )(qkv, qkv, qkv, lam_vec, sub_gain.reshape(1, LANES))


def _peer_scores_kernel(q_ref, keys_ref, o_ref):
    o_ref[...] = lax.dot_general(keys_ref[...], q_ref[...], (((1,), (1,)), ((), ())),
                                 precision=lax.Precision.HIGHEST, preferred_element_type=F32)


def _peer_scores(q, keys, tt):
    n = q.shape[0]
    hp, nk, half = keys.shape
    return pl.pallas_call(
        _peer_scores_kernel,
        grid=(n // tt, hp),
        in_specs=[
            pl.BlockSpec((tt, half), lambda i, j: (i, j)),
            pl.BlockSpec((None, nk, half), lambda i, j: (j, 0, 0)),
        ],
        out_specs=pl.BlockSpec((None, nk, tt), lambda i, j: (j, 0, i)),
        out_shape=jax.ShapeDtypeStruct((hp, nk, n), F32),
        compiler_params=_cparams(("arbitrary", "arbitrary")),
        name="peer_scores",
    )(q, keys)


SPLIT_B = 4


def _peer_select_kernel(s_ref, e1_ref, bi_ref, e2_ref, r2_ref, rank_ref, val_ref, sel_ref):
    nk, tl = s_ref.shape[1:]
    k = PEER_TOPK
    neg = -jnp.inf
    unranked = float(2 * k)

    def top(s, exact):
        key = lax.broadcasted_iota(jnp.int32, s.shape, 0)
        work = s
        rank = jnp.full(s.shape, unranked, F32)
        vals = []
        for r in range(k):
            m = jnp.max(work, axis=0, keepdims=True)
            hit = work == m
            if exact:
                hit = key == jnp.min(jnp.where(hit, key, nk), axis=0, keepdims=True)
            work = jnp.where(hit, neg, work)
            rank = jnp.where(hit, float(r), rank)
            vals.append(m)
        removed = jnp.sum(jnp.where(rank < unranked, 1.0, 0.0), axis=0, keepdims=True)
        return jnp.concatenate(vals, axis=0), rank, removed

    def stage1(exact):
        removed = jnp.zeros((1, tl), F32)
        for p in range(2):
            vals, rank, rem = top(s_ref[p], exact)
            val_ref[p] = vals
            rank_ref[p] = rank
            removed = jnp.maximum(removed, rem)
        return jnp.max(removed)

    sub = lax.broadcasted_iota(jnp.int32, (k, tl), 0)
    n_a = k // (SPLIT_B + 1)

    def stage2(exact):
        v1 = val_ref[0]
        v2 = val_ref[1]
        work, flat, valid = [], [], []
        for b in range(SPLIT_B):
            ok = sub <= k // (b + 1) - 1
            work.append(jnp.where(ok, v1 + v2[b:b + 1], neg))
            flat.append(jnp.where(ok, sub * k + b, -1))
            valid.append(ok)
        for a in range(n_a):
            ok = (sub >= SPLIT_B) & (sub <= k // (a + 1) - 1)
            work.append(jnp.where(ok, v1[a:a + 1] + v2, neg))
            flat.append(jnp.where(ok, sub + a * k, -1))
            valid.append(ok)
        top_sum = v1[0:1] + v2[0:1]
        z = jnp.zeros((1, tl), F32)
        for _ in range(k):
            m = jnp.max(functools.reduce(jnp.maximum, work), axis=0, keepdims=True)
            z = z + jnp.exp(m - top_sum)
            if exact:
                cand = functools.reduce(jnp.minimum, [jnp.where(w == m, f, k * k) for w, f in zip(work, flat)])
                idx = jnp.min(cand, axis=0, keepdims=True)
            for n in range(len(work)):
                hit = (flat[n] == idx) if exact else (work[n] == m)
                work[n] = jnp.where(hit, neg, work[n])
        count = [jnp.where(ok & (w == neg), 1.0, 0.0) for ok, w in zip(valid, work)]
        per_a = functools.reduce(jnp.add, count[:SPLIT_B])
        for a in range(n_a):
            extra = jnp.sum(count[SPLIT_B + a], axis=0, keepdims=True)
            per_a = per_a + jnp.where(sub == a, extra, 0.0)
        sel_ref[0:k, :] = per_a
        sel_ref[k:k + 1, :] = z
        return jnp.max(jnp.sum(per_a, axis=0, keepdims=True))

    @pl.when(stage1(False) > float(k))
    def _():
        stage1(True)

    @pl.when(stage2(False) > float(k))
    def _():
        stage2(True)

    s1 = s_ref[0]
    rank1 = rank_ref[0]
    bi = jnp.zeros(s1.shape, F32)
    for a in range(k):
        bi = jnp.where(rank1 == float(a), sel_ref[a:a + 1, :], bi)
    bi_ref[...] = bi
    e1_ref[...] = 0.5 * jnp.exp(s1 - val_ref[0, 0:1, :]) / sel_ref[k:k + 1, :]
    e2_ref[...] = jnp.exp(s_ref[1] - val_ref[1, 0:1, :]).astype(e2_ref.dtype)
    r2_ref[...] = rank_ref[1].astype(r2_ref.dtype)


def _peer_select(scores, heads, tl):
    hp, nk, n = scores.shape
    k = PEER_TOPK
    out = jax.ShapeDtypeStruct((heads, nk, n), F32)
    spec = pl.BlockSpec((None, nk, tl), lambda t, h: (h, 0, t))
    return pl.pallas_call(
        _peer_select_kernel,
        grid=(n // tl, heads),
        in_specs=[pl.BlockSpec((None, 2, nk, tl), lambda t, h: (h, 0, 0, t))],
        out_specs=[spec] * 4,
        out_shape=[out] * 4,
        scratch_shapes=[pltpu.VMEM((2, nk, tl), F32), pltpu.VMEM((2, k, tl), F32),
                        pltpu.VMEM((k + 8, tl), F32)],
        compiler_params=_cparams(("arbitrary", "arbitrary")),
        name="peer_select",
    )(scores.reshape(heads, 2, nk, n))


def _peer_dense_kernel(h_ref, u_ref, v_ref, e1_ref, bi_ref, e2_ref, r2_ref, o_ref, *, nk, chunk):
    e = pl.program_id(1)
    heads = e1_ref.shape[0]
    et = u_ref.shape[0]

    @pl.when(e == 0)
    def _():
        o_ref[...] = jnp.zeros_like(o_ref)

    total = None
    for c in range(et // chunk):
        a = lax.dot_general(u_ref[c * chunk:(c + 1) * chunk, :], h_ref[...], (((1,), (1,)), ((), ())),
                            preferred_element_type=F32)
        act = a * (1.0 + lax.erf(a * (2.0 ** -0.5)))
        parts = []
        for ii in range(chunk // nk):
            i = (e * et + c * chunk) // nk + ii
            g = jnp.zeros((nk, a.shape[1]), F32)
            for hh in range(heads):
                e1 = e1_ref[hh, pl.ds(i, 1), :]
                bi = bi_ref[hh, pl.ds(i, 1), :]
                g = g + jnp.where(r2_ref[hh] < bi, e2_ref[hh] * e1, 0.0)
            parts.append(g * act[ii * nk:(ii + 1) * nk])
        m = jnp.concatenate(parts, axis=0) if len(parts) > 1 else parts[0]
        contrib = jnp.dot(m.T.astype(BF16), v_ref[c * chunk:(c + 1) * chunk, :], preferred_element_type=F32)
        total = contrib if total is None else total + contrib
    o_ref[...] += total


def _peer_dense(h, u, v, e1, bi, e2, r2, tl, et, chunk):
    n, d = h.shape
    ne = u.shape[0]
    heads, nk, _ = e1.shape
    fac = pl.BlockSpec((heads, nk, tl), lambda t, e: (0, 0, t))
    return pl.pallas_call(
        functools.partial(_peer_dense_kernel, nk=nk, chunk=chunk),
        grid=(n // tl, ne // et),
        in_specs=[
            pl.BlockSpec((tl, d), lambda t, e: (t, 0)),
            pl.BlockSpec((et, d), lambda t, e: (e, 0)),
            pl.BlockSpec((et, d), lambda t, e: (e, 0)),
            fac, fac, fac, fac,
        ],
        out_specs=pl.BlockSpec((tl, d), lambda t, e: (t, 0)),
        out_shape=jax.ShapeDtypeStruct((n, d), F32),
        compiler_params=_cparams(("arbitrary", "arbitrary")),
        name="peer_dense",
    )(h, u, v, e1, bi, e2, r2)


def _gated_add_kernel(gid_ref, x_ref, y_ref, gate_ref, o_ref):
    del gid_ref
    o_ref[...] = x_ref[...] + gate_ref[...] * y_ref[...]


def _gated_add(x, y, gid, gate, tm):
    n, d = x.shape
    spec = pl.BlockSpec((tm, d), lambda i, g: (i, 0))
    return pl.pallas_call(
        _gated_add_kernel,
        grid_spec=pltpu.PrefetchScalarGridSpec(
            num_scalar_prefetch=1,
            grid=(n // tm,),
            in_specs=[spec, spec, pl.BlockSpec((None, 1, d), lambda i, g: (g[i], 0, 0))],
            out_specs=spec,
        ),
        out_shape=jax.ShapeDtypeStruct((n, d), F32),
        compiler_params=_cparams(("arbitrary",)),
        name="gated_add",
    )(gid, x, y, gate)


def _rope_tables(t_lat, t_ctx, batch):
    rows = t_lat // GRID_W
    row = jnp.repeat(jnp.arange(rows, dtype=jnp.int32), GRID_W)
    col = jnp.tile(jnp.arange(GRID_W, dtype=jnp.int32), rows)
    nfreq = HEAD_DIM // 4
    inv_freq = ROPE_THETA ** (-jnp.arange(nfreq, dtype=F32) / nfreq)
    ang_r = row.astype(F32)[:, None] * inv_freq
    ang_c = col.astype(F32)[:, None] * inv_freq
    cos = jnp.concatenate([jnp.cos(ang_r)] * 2 + [jnp.cos(ang_c)] * 2, axis=1)
    sin = jnp.concatenate([-jnp.sin(ang_r), jnp.sin(ang_r), -jnp.sin(ang_c), jnp.sin(ang_c)], axis=1)
    reps = LANES // HEAD_DIM
    cos = jnp.concatenate([jnp.ones((t_ctx, LANES), F32), jnp.tile(cos, (1, reps))], axis=0)
    sin = jnp.concatenate([jnp.zeros((t_ctx, LANES), F32), jnp.tile(sin, (1, reps))], axis=0)
    return jnp.tile(cos, (batch, 1)), jnp.tile(sin, (batch, 1))


def kernel(x, c, ctx, c_ctx, ada_w, ada_b, norm_gain, w_in, conv_w, rw_w0, rw_w_up, rw_a0, rw_a_up,
           rw_g_up, rw_k_k, rw_k_a, rw_r_k, rw_gn_gain, rw_gn_bias, da_q_gain, da_k_gain, da_lambda,
           da_sub_gain, w_out, peer_wq, peer_keys, peer_u, peer_v):
    batch, t_lat, d = x.shape
    t_ctx = ctx.shape[1]
    seq = t_ctx + t_lat
    n = batch * seq
    depth = ada_w.shape[0]
    rw = rw_k_k.shape[1]
    lora_w = rw_w_up.shape[2] + rw_a_up.shape[2] + rw_g_up.shape[1]
    da_w = w_out.shape[1] - rw
    p_heads, _, nk, half = peer_keys.shape[1:]
    assert 2 * batch * (rw // HEAD_DIM) == LANES, "the recurrence kernel fills the lanes with (half, batch, head)"

    tm = _tile(math.gcd(t_lat, t_ctx), 256)
    tile_pos = (jnp.arange(n // tm, dtype=jnp.int32) * tm) % seq
    gid = jnp.where(tile_pos < t_ctx, batch, jnp.arange(n // tm, dtype=jnp.int32) * tm // seq)
    first = ((tile_pos == 0) | (tile_pos == t_ctx)).astype(jnp.int32)
    last = ((tile_pos + tm == t_ctx) | (tile_pos + tm == seq)).astype(jnp.int32)

    xs = jnp.concatenate([ctx, x], axis=1).reshape(n, d)
    c8 = jnp.concatenate([c, c_ctx[None], jnp.zeros((8 - batch - 1, d), F32)], axis=0)
    mods = _modulation(c8, ada_w, ada_b)[:, :batch + 1].reshape(depth, batch + 1, 6, 1, d)
    cos_t, sin_t = _rope_tables(t_lat, t_ctx, batch)

    cols = jnp.concatenate([jnp.arange(3 * rw), jnp.arange(3 * rw + lora_w, w_in.shape[2]),
                            jnp.arange(3 * rw, 3 * rw + lora_w)])
    tn_in = _tile(w_in.shape[2], 1280, LANES)
    ts = _tile(tm, 128)

    for l in range(depth):
        lam_init = 0.8 - 0.6 * math.exp(-0.3 * l)
        mod = [mods[l, :, j] for j in range(6)]

        p = _norm_proj_wide(xs, gid, norm_gain[l, 0], mod[0], mod[1], w_in[l][:, cols].astype(BF16), tm,
                            _tile(n, 1024), tn_in)
        q8, v, bonus, gate = _rwkv_prep(
            p, first, last, conv_w[l], rw_w0[l], rw_w_up[l], rw_a0[l], rw_a_up[l], rw_g_up[l],
            rw_k_k[l], rw_k_a[l], rw_r_k[l].reshape(-1), tm)
        qs, vs = _to_streams(q8, v, batch, ts)
        yf, yb = _rwkv_scan(qs, vs, t_ctx, tc=_tile(math.gcd(t_lat, t_ctx), 32))
        rw_out = _rwkv_finish(yf, yb, bonus, gate, rw_gn_gain[l], rw_gn_bias[l], batch, ts)

        qkv = _da_prep(p, cos_t, sin_t, da_q_gain[l], da_k_gain[l], 3 * rw, da_w, tm)
        da_out = _diff_attention(qkv, da_lambda[l], da_sub_gain[l], lam_init, batch, t_ctx, tm)
        xs = _gated_proj(rw_out, da_out, w_out[l].astype(BF16), xs, gid, mod[2], tm)

        q, h = _norm_proj(xs, gid, norm_gain[l, 1], mod[3], mod[4], peer_wq[l].astype(BF16), tm,
                          _tile(peer_wq.shape[2], 2048, LANES), emit_h=True)
        scores = _peer_scores(q, peer_keys[l].reshape(2 * p_heads, nk, half), _tile(n, 1024, LANES))
        e1, bi, e2, r2 = _peer_select(scores, p_heads, _tile(n, 512, LANES))
        out = _peer_dense(h, peer_u[l].astype(BF16), peer_v[l].astype(BF16), e1, bi, e2, r2,
                          tl=_tile(n, 512, LANES), et=4 * nk, chunk=2 * nk)
        xs = _gated_add(xs, out, gid, mod[5], tm)

    return xs.reshape(batch, seq, d)[:, t_ctx:]
```

```python
import functools
import math

import jax
import jax.numpy as jnp
from jax import lax
from jax.experimental import pallas as pl
from jax.experimental.pallas import tpu as pltpu

F32 = jnp.float32
BF16 = jnp.bfloat16

HEAD_DIM = 64
HEAD_SHIFT = HEAD_DIM.bit_length() - 1
ROPE_SHIFT = (HEAD_DIM // 4).bit_length() - 1
LANES = 128
GRID_W = 64
ROPE_THETA = 10000.0
NORM_EPS = 1e-6
GN_EPS = 64e-5
L2_EPS = 1e-12
PEER_TOPK = 16
LOG2_E = 1.4426950408889634
VMEM_LIMIT = 56 * 1024 * 1024


def _cparams(sem):
    return pltpu.CompilerParams(dimension_semantics=sem, vmem_limit_bytes=VMEM_LIMIT)


def _mm(a, b):
    return jnp.dot(a.astype(BF16), b.astype(BF16), preferred_element_type=F32)


def _segment_ones():
    r = lax.broadcasted_iota(jnp.int32, (LANES, LANES), 0) >> HEAD_SHIFT
    c = lax.broadcasted_iota(jnp.int32, (LANES, LANES), 1) >> HEAD_SHIFT
    return jnp.where(r == c, 1.0, 0.0).astype(BF16)


def _segsum(x, ones_bd):
    hi = x.astype(BF16)
    lo = (x - hi.astype(F32)).astype(BF16)
    return (jnp.dot(hi, ones_bd, preferred_element_type=F32)
            + jnp.dot(lo, ones_bd, preferred_element_type=F32))


def _tile(n, pref, align=8):
    for t in range(min(n, pref) // align * align, 0, -align):
        if n % t == 0:
            return t
    return n


def _mod_kernel(c_ref, w_ref, b_ref, o_ref):
    cv = c_ref[...]
    act = cv * jax.nn.sigmoid(cv)
    o_ref[...] = _mm(act, w_ref[...]) + b_ref[...]


def _modulation(c8, ada_w, ada_b):
    depth, d, n6 = ada_w.shape
    tn = _tile(n6, 1536)
    return pl.pallas_call(
        _mod_kernel,
        grid=(depth, n6 // tn),
        in_specs=[
            pl.BlockSpec((8, d), lambda l, j: (0, 0)),
            pl.BlockSpec((None, d, tn), lambda l, j: (l, 0, j)),
            pl.BlockSpec((None, 1, tn), lambda l, j: (l, 0, j)),
        ],
        out_specs=pl.BlockSpec((None, 8, tn), lambda l, j: (l, 0, j)),
        out_shape=jax.ShapeDtypeStruct((depth, 8, n6), F32),
        compiler_params=_cparams(("arbitrary", "arbitrary")),
        name="modulation",
    )(c8, ada_w, ada_b.reshape(depth, 1, n6))


def _norm_mod_kernel(gid_ref, x_ref, gain_ref, shift_ref, scale_ref, o_ref):
    del gid_ref
    x = x_ref[...]
    ms = jnp.mean(x * x, axis=-1, keepdims=True)
    y = x * lax.rsqrt(ms + NORM_EPS) * gain_ref[...]
    o_ref[...] = (y * (1.0 + scale_ref[...]) + shift_ref[...]).astype(BF16)


def _matmul_kernel(h_ref, w_ref, o_ref):
    o_ref[...] = jnp.dot(h_ref[...], w_ref[...], preferred_element_type=F32)


def _norm_proj_wide(x, gid, gain, shift, scale, w, tm, tm_mm, tn):
    n, d = x.shape
    nout = w.shape[1]
    h = pl.pallas_call(
        _norm_mod_kernel,
        grid_spec=pltpu.PrefetchScalarGridSpec(
            num_scalar_prefetch=1,
            grid=(n // tm,),
            in_specs=[
                pl.BlockSpec((tm, d), lambda i, g: (i, 0)),
                pl.BlockSpec((1, d), lambda i, g: (0, 0)),
                pl.BlockSpec((None, 1, d), lambda i, g: (g[i], 0, 0)),
                pl.BlockSpec((None, 1, d), lambda i, g: (g[i], 0, 0)),
            ],
            out_specs=pl.BlockSpec((tm, d), lambda i, g: (i, 0)),
        ),
        out_shape=jax.ShapeDtypeStruct((n, d), BF16),
        compiler_params=_cparams(("arbitrary",)),
        name="norm_mod",
    )(gid, x, gain.reshape(1, d), shift, scale)
    return pl.pallas_call(
        _matmul_kernel,
        grid=(nout // tn, n // tm_mm),
        in_specs=[
            pl.BlockSpec((tm_mm, d), lambda j, i: (i, 0)),
            pl.BlockSpec((d, tn), lambda j, i: (0, j)),
        ],
        out_specs=pl.BlockSpec((tm_mm, tn), lambda j, i: (i, j)),
        out_shape=jax.ShapeDtypeStruct((n, nout), F32),
        compiler_params=_cparams(("arbitrary", "arbitrary")),
        name="proj_matmul",
    )(h, w)


def _gated_proj_kernel(gid_ref, y1_ref, y2_ref, w1_ref, w2_ref, x_ref, gate_ref, o_ref):
    del gid_ref
    mix = (jnp.dot(y1_ref[...], w1_ref[...], preferred_element_type=F32)
           + jnp.dot(y2_ref[...], w2_ref[...], preferred_element_type=F32))
    o_ref[...] = x_ref[...] + gate_ref[...] * mix


def _gated_proj(y1, y2, w, x, gid, gate, tm):
    n, k1 = y1.shape
    k2 = y2.shape[1]
    d = w.shape[1]
    return pl.pallas_call(
        _gated_proj_kernel,
        grid_spec=pltpu.PrefetchScalarGridSpec(
            num_scalar_prefetch=1,
            grid=(n // tm,),
            in_specs=[
                pl.BlockSpec((tm, k1), lambda i, g: (i, 0)),
                pl.BlockSpec((tm, k2), lambda i, g: (i, 0)),
                pl.BlockSpec((k1, d), lambda i, g: (0, 0)),
                pl.BlockSpec((k2, d), lambda i, g: (k1 // k2, 0)),
                pl.BlockSpec((tm, d), lambda i, g: (i, 0)),
                pl.BlockSpec((None, 1, d), lambda i, g: (g[i], 0, 0)),
            ],
            out_specs=pl.BlockSpec((tm, d), lambda i, g: (i, 0)),
        ),
        out_shape=jax.ShapeDtypeStruct((n, d), F32),
        compiler_params=_cparams(("arbitrary",)),
        name="gated_proj",
    )(gid, y1, y2, w, w, x, gate)


def _rwkv_prep_kernel(first_ref, last_ref, x_ref, prev_ref, next_ref, tail_ref, cw_ref,
                      w0_ref, wup_ref, a0_ref, aup_ref, gup_ref, kk_ref, ka_ref, rk_ref,
                      q_out, v_out, bonus_out, gate_out, *, rw, lw, la):
    i = pl.program_id(0)
    tm = x_ref.shape[0]
    x = x_ref[...]
    row = lax.broadcasted_iota(jnp.int32, x.shape, 0)
    keep_prev = jnp.where(first_ref[i] == 1, 0.0, 1.0)
    keep_next = jnp.where(last_ref[i] == 1, 0.0, 1.0)
    halo_prev = prev_ref[7:8, :] * keep_prev
    halo_next = next_ref[0:1, :] * keep_next
    x_prev = jnp.where(row == 0, halo_prev, pltpu.roll(x, 1, axis=0))
    x_next = jnp.where(row == tm - 1, halo_next, pltpu.roll(x, tm - 1, axis=0))
    conv = x_prev * cw_ref[0:1, :] + x * cw_ref[1:2, :] + x_next * cw_ref[2:3, :]
    r = conv[:, 0:rw]
    k = conv[:, rw:2 * rw]
    v = conv[:, 2 * rw:3 * rw]
    tail = tail_ref[...]
    w_lo = jnp.tanh(tail[:, 0:lw])
    a_lo = tail[:, lw:lw + la]
    g_lo = jax.nn.sigmoid(tail[:, lw + la:])
    ones_bd = _segment_ones()

    def seg(t):
        return jnp.concatenate(
            [_segsum(t[:, c * LANES:(c + 1) * LANES], ones_bd) for c in range(rw // LANES)], axis=1)

    kap = k * kk_ref[...]
    nrm = jnp.sqrt(seg(kap * kap))
    kap = kap / jnp.maximum(nrm, L2_EPS)
    q_out[0] = r
    q_out[1] = kap
    v_out[...] = v
    gate_out[...] = _mm(g_lo, gup_ref[...])
    bonus = jnp.zeros_like(r)
    for d in range(2):
        z = w0_ref[d:d + 1, :] + _mm(w_lo, wup_ref[d])
        log_w = -jax.nn.softplus(-z) - 0.5
        q_out[2 + d] = jnp.exp(-jnp.exp(log_w))
        a = jax.nn.sigmoid(a0_ref[d:d + 1, :] + _mm(a_lo, aup_ref[d]))
        k_mod = k * (1.0 + (a - 1.0) * ka_ref[...])
        q_out[4 + d] = k_mod
        q_out[6 + d] = kap * a
        bonus = bonus + seg(r * k_mod * rk_ref[...]) * v
    bonus_out[...] = bonus


def _rwkv_prep(p, first, last, conv_w, w0, w_up, a0, a_up, g_up, k_k, k_a, r_k, tm):
    n = p.shape[0]
    rw = k_k.shape[0]
    lw, la, lg = w_up.shape[1], a_up.shape[1], g_up.shape[0]
    tail_w = lw + la + lg
    tail_blk = (p.shape[1] - tail_w) // tail_w
    nb8 = n // 8
    row2 = lambda i, f, l: (0, 0)
    row3 = lambda i, f, l: (0, 0, 0)
    one = jax.ShapeDtypeStruct((n, rw), F32)
    one_spec = pl.BlockSpec((tm, rw), lambda i, f, l: (i, 0))
    return pl.pallas_call(
        functools.partial(_rwkv_prep_kernel, rw=rw, lw=lw, la=la),
        grid_spec=pltpu.PrefetchScalarGridSpec(
            num_scalar_prefetch=2,
            grid=(n // tm,),
            in_specs=[
                pl.BlockSpec((tm, 3 * rw), lambda i, f, l: (i, 0)),
                pl.BlockSpec((8, 3 * rw), lambda i, f, l: (jnp.maximum(i * (tm // 8) - 1, 0), 0)),
                pl.BlockSpec((8, 3 * rw), lambda i, f, l: (jnp.minimum((i + 1) * (tm // 8), nb8 - 1), 0)),
                pl.BlockSpec((tm, tail_w), lambda i, f, l: (i, tail_blk)),
                pl.BlockSpec((3, 3 * rw), row2),
                pl.BlockSpec((2, rw), row2),
                pl.BlockSpec((2, lw, rw), row3),
                pl.BlockSpec((2, rw), row2),
                pl.BlockSpec((2, la, rw), row3),
                pl.BlockSpec((lg, rw), row2),
                pl.BlockSpec((1, rw), row2),
                pl.BlockSpec((1, rw), row2),
                pl.BlockSpec((1, rw), row2),
            ],
            out_specs=[pl.BlockSpec((8, tm, rw), lambda i, f, l: (0, i, 0)), one_spec, one_spec, one_spec],
        ),
        out_shape=[jax.ShapeDtypeStruct((8, n, rw), F32), one, one, one],
        compiler_params=_cparams(("arbitrary",)),
        name="rwkv_prep",
    )(first, last, p, p, p, p, conv_w, w0, w_up, a0, a_up, g_up,
      k_k.reshape(1, rw), k_a.reshape(1, rw), r_k.reshape(1, rw))


def _to_streams_kernel(x_ref, o_ref, xt_ref, *, split_value):
    batch, tm, rw = x_ref.shape
    heads = rw // HEAD_DIM
    for b in range(batch):
        xt_ref[b] = x_ref[b].T
    rows = HEAD_DIM // 2 if split_value else HEAD_DIM
    for j in range(rows):
        second = j + HEAD_DIM // 2 if split_value else j
        parts = [xt_ref[b, pl.ds(j, heads, stride=HEAD_DIM), :] for b in range(batch)]
        parts += [xt_ref[b, pl.ds(second, heads, stride=HEAD_DIM), :] for b in range(batch)]
        slab = jnp.concatenate(parts, axis=0).T
        if split_value:
            o_ref[:, j, :] = slab
        else:
            o_ref[j] = slab


def _to_streams(q, v, batch, tm):
    nq, n, rw = q.shape
    seq = n // batch
    scratch = [pltpu.VMEM((batch, rw, tm), F32)]
    qs = pl.pallas_call(
        functools.partial(_to_streams_kernel, split_value=False),
        grid=(nq, seq // tm),
        in_specs=[pl.BlockSpec((None, batch, tm, rw), lambda a, i: (a, 0, i, 0))],
        out_specs=pl.BlockSpec((None, HEAD_DIM, tm, LANES), lambda a, i: (a, 0, i, 0)),
        out_shape=jax.ShapeDtypeStruct((nq, HEAD_DIM, seq, LANES), F32),
        scratch_shapes=scratch,
        compiler_params=_cparams(("arbitrary", "arbitrary")),
        name="to_streams",
    )(q.reshape(nq, batch, seq, rw))
    vs = pl.pallas_call(
        functools.partial(_to_streams_kernel, split_value=True),
        grid=(seq // tm,),
        in_specs=[pl.BlockSpec((batch, tm, rw), lambda i: (0, i, 0))],
        out_specs=pl.BlockSpec((tm, HEAD_DIM // 2, LANES), lambda i: (i, 0, 0)),
        out_shape=jax.ShapeDtypeStruct((seq, HEAD_DIM // 2, LANES), F32),
        scratch_shapes=scratch,
        compiler_params=_cparams(("arbitrary",)),
        name="to_streams_v",
    )(v.reshape(batch, seq, rw))
    return qs, vs


def _rwkv_scan_kernel(rf, kapf, wf, kf, bf, vf, rb, kapb, wb, kb, bb, vb, yf, yb, s_ref):
    hd, tc, ns = rf.shape
    half = vf.shape[1]

    @pl.when(pl.program_id(0) == 0)
    def _():
        s_ref[...] = jnp.zeros_like(s_ref)

    dirs = ((rf, kapf, wf, kf, bf, vf, yf), (rb, kapb, wb, kb, bb, vb, yb))

    def state_times_kappa(d, t):
        acc = jnp.zeros((half, ns), F32)
        for kk in range(hd):
            acc = acc + s_ref[d, kk] * dirs[d][1][kk, pl.ds(t, 1), :]
        return acc

    def update(d, t, t_next, sa):
        r_ref, kap_ref, w_ref, k_ref, b_ref, v_ref, y_ref = dirs[d]
        vv = v_ref[t]
        y = jnp.zeros((half, ns), F32)
        sa_next = jnp.zeros((half, ns), F32)
        for kk in range(hd):
            s_new = (s_ref[d, kk] * w_ref[kk, pl.ds(t, 1), :]
                     + (vv * k_ref[kk, pl.ds(t, 1), :] - sa * b_ref[kk, pl.ds(t, 1), :]))
            s_ref[d, kk] = s_new
            y = y + s_new * r_ref[kk, pl.ds(t, 1), :]
            sa_next = sa_next + s_new * kap_ref[kk, pl.ds(t_next, 1), :]
        y_ref[t] = y
        return sa_next

    def step(j, carry):
        sa_f, sa_b = carry
        sa_f = update(0, j, jnp.minimum(j + 1, tc - 1), sa_f)
        tb = tc - 1 - j
        sa_b = update(1, tb, jnp.maximum(tb - 1, 0), sa_b)
        return sa_f, sa_b

    lax.fori_loop(0, tc, step, (state_times_kappa(0, 0), state_times_kappa(1, tc - 1)))


def _rwkv_scan(qs, vs, t_ctx, tc):
    _, hd, seq, ns = qs.shape
    half = vs.shape[1]
    nb = seq // tc
    nbc = t_ctx // tc

    def back_block(i):
        return jnp.where(i < nbc, nbc - 1 - i, nb - 1 - (i - nbc))

    def qspec(a, back):
        if back:
            return pl.BlockSpec((None, hd, tc, ns), lambda i: (a, 0, back_block(i), 0))
        return pl.BlockSpec((None, hd, tc, ns), lambda i: (a, 0, i, 0))

    def vspec(back):
        if back:
            return pl.BlockSpec((tc, half, ns), lambda i: (back_block(i), 0, 0))
        return pl.BlockSpec((tc, half, ns), lambda i: (i, 0, 0))

    in_specs = ([qspec(a, False) for a in (0, 1, 2, 4, 6)] + [vspec(False)]
                + [qspec(a, True) for a in (0, 1, 3, 5, 7)] + [vspec(True)])
    out = jax.ShapeDtypeStruct((seq, half, ns), F32)
    return pl.pallas_call(
        _rwkv_scan_kernel,
        grid=(nb,),
        in_specs=in_specs,
        out_specs=[vspec(False), vspec(True)],
        out_shape=[out, out],
        scratch_shapes=[pltpu.VMEM((2, hd, half, ns), F32)],
        compiler_params=_cparams(("arbitrary",)),
        name="rwkv_scan",
    )(*([qs] * 5 + [vs] + [qs] * 5 + [vs]))


def _rwkv_finish_kernel(yf_ref, yb_ref, bonus_ref, gate_ref, gain_ref, bias_ref, o_ref, xt_ref):
    batch, tm, rw = bonus_ref.shape
    heads = rw // HEAD_DIM
    half = yf_ref.shape[1]
    for j in range(half):
        slab = (yf_ref[:, j, :] + yb_ref[:, j, :]).T
        for g in range(2 * batch):
            xt_ref[g % batch, pl.ds(j + half * (g // batch), heads, stride=HEAD_DIM), :] = (
                slab[g * heads:(g + 1) * heads])
    ones_bd = _segment_ones()
    for b in range(batch):
        y = xt_ref[b].T
        outs = []
        for c in range(rw // LANES):
            yc = y[:, c * LANES:(c + 1) * LANES]
            mu = _segsum(yc, ones_bd) * (1.0 / HEAD_DIM)
            dev = yc - mu
            var = _segsum(dev * dev, ones_bd) * (1.0 / HEAD_DIM)
            outs.append(dev * lax.rsqrt(var + GN_EPS))
        yn = jnp.concatenate(outs, axis=1)
        o_ref[b] = ((yn * gain_ref[...] + bias_ref[...] + bonus_ref[b]) * gate_ref[b]).astype(o_ref.dtype)


def _rwkv_finish(yf, yb, bonus, gate, gain, bias, batch, tm):
    seq, half, ns = yf.shape
    n, rw = bonus.shape
    yspec = pl.BlockSpec((tm, half, ns), lambda i: (i, 0, 0))
    spec = pl.BlockSpec((batch, tm, rw), lambda i: (0, i, 0))
    vec = pl.BlockSpec((1, rw), lambda i: (0, 0))
    out = pl.pallas_call(
        _rwkv_finish_kernel,
        grid=(seq // tm,),
        in_specs=[yspec, yspec, spec, spec, vec, vec],
        out_specs=spec,
        out_shape=jax.ShapeDtypeStruct((batch, seq, rw), BF16),
        scratch_shapes=[pltpu.VMEM((batch, rw, tm), F32)],
        compiler_params=_cparams(("arbitrary",)),
        name="rwkv_finish",
    )(yf, yb, bonus.reshape(batch, seq, rw), gate.reshape(batch, seq, rw),
      gain.reshape(1, rw), bias.reshape(1, rw))
    return out.reshape(n, rw)


def _da_prep_kernel(x_ref, cos_ref, sin_ref, qg_ref, kg_ref, o_ref):
    s = pl.program_id(1)

    @pl.when(s == 2)
    def _():
        o_ref[...] = x_ref[...].astype(BF16)

    @pl.when(s < 2)
    def _():
        ones_bd = _segment_ones()
        gain = jnp.where(s == 0, qg_ref[...] * (HEAD_DIM ** -0.5 * LOG2_E), kg_ref[...])
        cos = cos_ref[...]
        sin = sin_ref[...]
        lane = lax.broadcasted_iota(jnp.int32, cos.shape, 1)
        even = ((lane >> ROPE_SHIFT) & 1) == 0
        x = x_ref[...]
        for c in range(x.shape[1] // LANES):
            xc = x[:, c * LANES:(c + 1) * LANES]
            ms = _segsum(xc * xc, ones_bd) * (1.0 / HEAD_DIM)
            xn = xc * lax.rsqrt(ms + NORM_EPS) * gain
            swapped = jnp.where(even, pltpu.roll(xn, LANES - HEAD_DIM // 4, axis=1),
                                pltpu.roll(xn, HEAD_DIM // 4, axis=1))
            o_ref[:, c * LANES:(c + 1) * LANES] = (xn * cos + swapped * sin).astype(BF16)


def _da_prep(p, cos_t, sin_t, q_gain, k_gain, col0, width, tm):
    n = p.shape[0]
    blk0 = col0 // width
    qg = jnp.tile(q_gain, LANES // HEAD_DIM).reshape(1, LANES)
    kg = jnp.tile(k_gain, LANES // HEAD_DIM).reshape(1, LANES)
    return pl.pallas_call(
        _da_prep_kernel,
        grid=(n // tm, 3),
        in_specs=[
            pl.BlockSpec((tm, width), lambda i, s: (i, blk0 + s)),
            pl.BlockSpec((tm, LANES), lambda i, s: (i, 0)),
            pl.BlockSpec((tm, LANES), lambda i, s: (i, 0)),
            pl.BlockSpec((1, LANES), lambda i, s: (0, 0)),
            pl.BlockSpec((1, LANES), lambda i, s: (0, 0)),
        ],
        out_specs=pl.BlockSpec((None, tm, width), lambda i, s: (s, i, 0)),
        out_shape=jax.ShapeDtypeStruct((3, n, width), BF16),
        compiler_params=_cparams(("arbitrary", "arbitrary")),
        name="da_prep",
    )(p, cos_t, sin_t, qg, kg)


def _diff_attn_kernel(q_ref, k_ref, v_ref, lam_ref, sg_ref, o_ref, *, lam_init, t_ctx):
    tq = q_ref.shape[0]
    lv = lam_ref[...]
    lam = (jnp.exp(jnp.sum(lv[0:1] * lv[1:2], axis=1, keepdims=True))
           - jnp.exp(jnp.sum(lv[2:3] * lv[3:4], axis=1, keepdims=True)) + lam_init)

    def attend(keys, vals):
        q = q_ref[...]
        lane = lax.broadcasted_iota(jnp.int32, q.shape, 1)
        zero = jnp.zeros_like(q)
        outs = []
        for m in range(2):
            qm = jnp.where((lane >> HEAD_SHIFT) == m, q, zero)
            s = lax.dot_general(qm, keys, (((1,), (1,)), ((), ())), preferred_element_type=F32)
            p = jnp.exp2(s - jnp.max(s, axis=1, keepdims=True))
            den = jnp.sum(p, axis=1, keepdims=True)
            outs.append(jnp.dot(p.astype(BF16), vals, preferred_element_type=F32) / den)
        o = outs[0] - lam * outs[1]
        ms = jnp.mean(o * o, axis=1, keepdims=True)
        o_ref[...] = (o * lax.rsqrt(ms + NORM_EPS) * sg_ref[...] * (1.0 - lam_init)).astype(o_ref.dtype)

    is_ctx = pl.program_id(2) < t_ctx // tq

    @pl.when(is_ctx)
    def _():
        attend(k_ref[0:t_ctx, :], v_ref[0:t_ctx, :])

    @pl.when(jnp.logical_not(is_ctx))
    def _():
        attend(k_ref[...], v_ref[...])


def _diff_attention(qkv, lam_vec, sub_gain, lam_init, batch, t_ctx, tq):
    _, n, width = qkv.shape
    heads = width // LANES
    seq = n // batch
    nq = seq // tq
    return pl.pallas_call(
        functools.partial(_diff_attn_kernel, lam_init=lam_init, t_ctx=t_ctx),
        grid=(batch, heads, nq),
        in_specs=[
            pl.BlockSpec((None, tq, LANES), lambda b, h, i: (0, b * nq + i, h)),
            pl.BlockSpec((None, seq, LANES), lambda b, h, i: (1, b, h)),
            pl.BlockSpec((None, seq, LANES), lambda b, h, i: (2, b, h)),
            pl.BlockSpec((4, HEAD_DIM), lambda b, h, i: (0, 0)),
            pl.BlockSpec((1, LANES), lambda b, h, i: (0, 0)),
        ],
        out_specs=pl.BlockSpec((tq, LANES), lambda b, h, i: (b * nq + i, h)),
        out_shape=jax.ShapeDtypeStruct((n, width), BF16),
        compiler_params=_cparams(("arbitrary", "arbitrary", "arbitrary")),
        name="diff_attn",
    )(qkv, qkv, qkv, lam_vec, sub_gain.reshape(1, LANES))


def _peer_query_kernel(gid_ref, x_ref, gain_ref, shift_ref, scale_ref, w_ref, keys_ref, h_ref, s_ref):
    del gid_ref
    x = x_ref[...]
    ms = jnp.mean(x * x, axis=-1, keepdims=True)
    y = x * lax.rsqrt(ms + NORM_EPS) * gain_ref[...]
    h = (y * (1.0 + scale_ref[...]) + shift_ref[...]).astype(BF16)
    h_ref[...] = h
    q = jnp.dot(h, w_ref[...], preferred_element_type=F32)
    half = keys_ref.shape[2]
    for j in range(keys_ref.shape[0]):
        s_ref[j] = lax.dot_general(keys_ref[j], q[:, j * half:(j + 1) * half], (((1,), (1,)), ((), ())),
                                   precision=lax.Precision.HIGHEST, preferred_element_type=F32)


def _peer_query(x, gid, gain, shift, scale, w, keys, tm):
    n, d = x.shape
    hp, nk, half = keys.shape
    return pl.pallas_call(
        _peer_query_kernel,
        grid_spec=pltpu.PrefetchScalarGridSpec(
            num_scalar_prefetch=1,
            grid=(n // tm,),
            in_specs=[
                pl.BlockSpec((tm, d), lambda i, g: (i, 0)),
                pl.BlockSpec((1, d), lambda i, g: (0, 0)),
                pl.BlockSpec((None, 1, d), lambda i, g: (g[i], 0, 0)),
                pl.BlockSpec((None, 1, d), lambda i, g: (g[i], 0, 0)),
                pl.BlockSpec((d, hp * half), lambda i, g: (0, 0)),
                pl.BlockSpec((hp, nk, half), lambda i, g: (0, 0, 0)),
            ],
            out_specs=[pl.BlockSpec((tm, d), lambda i, g: (i, 0)),
                       pl.BlockSpec((hp, nk, tm), lambda i, g: (0, 0, i))],
        ),
        out_shape=[jax.ShapeDtypeStruct((n, d), BF16), jax.ShapeDtypeStruct((hp, nk, n), F32)],
        compiler_params=_cparams(("arbitrary",)),
        name="peer_query",
    )(gid, x, gain.reshape(1, d), shift, scale, w, keys)


SPLIT_B = 4


def _peer_select_kernel(s_ref, e1_ref, bi_ref, e2_ref, r2_ref, rank_ref, val_ref, sel_ref):
    nk, tl = s_ref.shape[1:]
    k = PEER_TOPK
    neg = -jnp.inf
    unranked = float(2 * k)

    def top(s, exact):
        key = lax.broadcasted_iota(jnp.int32, s.shape, 0)
        work = s
        rank = jnp.full(s.shape, unranked, F32)
        vals = []
        for r in range(k):
            m = jnp.max(work, axis=0, keepdims=True)
            hit = work == m
            if exact:
                hit = key == jnp.min(jnp.where(hit, key, nk), axis=0, keepdims=True)
            work = jnp.where(hit, neg, work)
            rank = jnp.where(hit, float(r), rank)
            vals.append(m)
        removed = jnp.sum(jnp.where(rank < unranked, 1.0, 0.0), axis=0, keepdims=True)
        return jnp.concatenate(vals, axis=0), rank, removed

    def stage1(exact):
        removed = jnp.zeros((1, tl), F32)
        for p in range(2):
            vals, rank, rem = top(s_ref[p], exact)
            val_ref[p] = vals
            rank_ref[p] = rank
            removed = jnp.maximum(removed, rem)
        return jnp.max(removed)

    sub = lax.broadcasted_iota(jnp.int32, (k, tl), 0)
    n_a = k // (SPLIT_B + 1)

    def stage2(exact):
        v1 = val_ref[0]
        v2 = val_ref[1]
        work, flat, valid = [], [], []
        for b in range(SPLIT_B):
            ok = sub <= k // (b + 1) - 1
            work.append(jnp.where(ok, v1 + v2[b:b + 1], neg))
            flat.append(jnp.where(ok, sub * k + b, -1))
            valid.append(ok)
        for a in range(n_a):
            ok = (sub >= SPLIT_B) & (sub <= k // (a + 1) - 1)
            work.append(jnp.where(ok, v1[a:a + 1] + v2, neg))
            flat.append(jnp.where(ok, sub + a * k, -1))
            valid.append(ok)
        top_sum = v1[0:1] + v2[0:1]
        z = jnp.zeros((1, tl), F32)
        for _ in range(k):
            m = jnp.max(functools.reduce(jnp.maximum, work), axis=0, keepdims=True)
            z = z + jnp.exp(m - top_sum)
            if exact:
                cand = functools.reduce(jnp.minimum, [jnp.where(w == m, f, k * k) for w, f in zip(work, flat)])
                idx = jnp.min(cand, axis=0, keepdims=True)
            for n in range(len(work)):
                hit = (flat[n] == idx) if exact else (work[n] == m)
                work[n] = jnp.where(hit, neg, work[n])
        count = [jnp.where(ok & (w == neg), 1.0, 0.0) for ok, w in zip(valid, work)]
        per_a = functools.reduce(jnp.add, count[:SPLIT_B])
        for a in range(n_a):
            extra = jnp.sum(count[SPLIT_B + a], axis=0, keepdims=True)
            per_a = per_a + jnp.where(sub == a, extra, 0.0)
        sel_ref[0:k, :] = per_a
        sel_ref[k:k + 1, :] = z
        return jnp.max(jnp.sum(per_a, axis=0, keepdims=True))

    @pl.when(stage1(False) > float(k))
    def _():
        stage1(True)

    @pl.when(stage2(False) > float(k))
    def _():
        stage2(True)

    s1 = s_ref[0]
    rank1 = rank_ref[0]
    bi = jnp.zeros(s1.shape, F32)
    for a in range(k):
        bi = jnp.where(rank1 == float(a), sel_ref[a:a + 1, :], bi)
    bi_ref[...] = bi
    e1_ref[...] = 0.5 * jnp.exp(s1 - val_ref[0, 0:1, :]) / sel_ref[k:k + 1, :]
    e2_ref[...] = jnp.exp(s_ref[1] - val_ref[1, 0:1, :]).astype(e2_ref.dtype)
    r2_ref[...] = rank_ref[1].astype(r2_ref.dtype)


def _peer_select(scores, heads, tl):
    hp, nk, n = scores.shape
    k = PEER_TOPK
    out = jax.ShapeDtypeStruct((heads, nk, n), F32)
    spec = pl.BlockSpec((None, nk, tl), lambda t, h: (h, 0, t))
    return pl.pallas_call(
        _peer_select_kernel,
        grid=(n // tl, heads),
        in_specs=[pl.BlockSpec((None, 2, nk, tl), lambda t, h: (h, 0, 0, t))],
        out_specs=[spec] * 4,
        out_shape=[out] * 4,
        scratch_shapes=[pltpu.VMEM((2, nk, tl), F32), pltpu.VMEM((2, k, tl), F32),
                        pltpu.VMEM((k + 8, tl), F32)],
        compiler_params=_cparams(("arbitrary", "arbitrary")),
        name="peer_select",
    )(scores.reshape(heads, 2, nk, n))


def _peer_dense_kernel(h_ref, u_ref, v_ref, e1_ref, bi_ref, e2_ref, r2_ref, o_ref, *, nk, chunk):
    e = pl.program_id(1)
    heads = e1_ref.shape[0]
    et = u_ref.shape[0]

    @pl.when(e == 0)
    def _():
        o_ref[...] = jnp.zeros_like(o_ref)

    total = None
    for c in range(et // chunk):
        a = lax.dot_general(u_ref[c * chunk:(c + 1) * chunk, :], h_ref[...], (((1,), (1,)), ((), ())),
                            preferred_element_type=F32)
        act = a * (1.0 + lax.erf(a * (2.0 ** -0.5)))
        parts = []
        for ii in range(chunk // nk):
            i = (e * et + c * chunk) // nk + ii
            g = jnp.zeros((nk, a.shape[1]), F32)
            for hh in range(heads):
                e1 = e1_ref[hh, pl.ds(i, 1), :]
                bi = bi_ref[hh, pl.ds(i, 1), :]
                g = g + jnp.where(r2_ref[hh] < bi, e2_ref[hh] * e1, 0.0)
            parts.append(g * act[ii * nk:(ii + 1) * nk])
        m = jnp.concatenate(parts, axis=0) if len(parts) > 1 else parts[0]
        contrib = jnp.dot(m.T.astype(BF16), v_ref[c * chunk:(c + 1) * chunk, :], preferred_element_type=F32)
        total = contrib if total is None else total + contrib
    o_ref[...] += total


def _peer_dense(h, u, v, e1, bi, e2, r2, tl, et, chunk):
    n, d = h.shape
    ne = u.shape[0]
    heads, nk, _ = e1.shape
    fac = pl.BlockSpec((heads, nk, tl), lambda t, e: (0, 0, t))
    return pl.pallas_call(
        functools.partial(_peer_dense_kernel, nk=nk, chunk=chunk),
        grid=(n // tl, ne // et),
        in_specs=[
            pl.BlockSpec((tl, d), lambda t, e: (t, 0)),
            pl.BlockSpec((et, d), lambda t, e: (e, 0)),
            pl.BlockSpec((et, d), lambda t, e: (e, 0)),
            fac, fac, fac, fac,
        ],
        out_specs=pl.BlockSpec((tl, d), lambda t, e: (t, 0)),
        out_shape=jax.ShapeDtypeStruct((n, d), F32),
        compiler_params=_cparams(("arbitrary", "arbitrary")),
        name="peer_dense",
    )(h, u, v, e1, bi, e2, r2)


def _gated_add_kernel(gid_ref, x_ref, y_ref, gate_ref, o_ref):
    del gid_ref
    o_ref[...] = x_ref[...] + gate_ref[...] * y_ref[...]


def _gated_add(x, y, gid, gate, tm):
    n, d = x.shape
    spec = pl.BlockSpec((tm, d), lambda i, g: (i, 0))
    return pl.pallas_call(
        _gated_add_kernel,
        grid_spec=pltpu.PrefetchScalarGridSpec(
            num_scalar_prefetch=1,
            grid=(n // tm,),
            in_specs=[spec, spec, pl.BlockSpec((None, 1, d), lambda i, g: (g[i], 0, 0))],
            out_specs=spec,
        ),
        out_shape=jax.ShapeDtypeStruct((n, d), F32),
        compiler_params=_cparams(("arbitrary",)),
        name="gated_add",
    )(gid, x, y, gate)


def _rope_tables(t_lat, t_ctx, batch):
    rows = t_lat // GRID_W
    row = jnp.repeat(jnp.arange(rows, dtype=jnp.int32), GRID_W)
    col = jnp.tile(jnp.arange(GRID_W, dtype=jnp.int32), rows)
    nfreq = HEAD_DIM // 4
    inv_freq = ROPE_THETA ** (-jnp.arange(nfreq, dtype=F32) / nfreq)
    ang_r = row.astype(F32)[:, None] * inv_freq
    ang_c = col.astype(F32)[:, None] * inv_freq
    cos = jnp.concatenate([jnp.cos(ang_r)] * 2 + [jnp.cos(ang_c)] * 2, axis=1)
    sin = jnp.concatenate([-jnp.sin(ang_r), jnp.sin(ang_r), -jnp.sin(ang_c), jnp.sin(ang_c)], axis=1)
    reps = LANES // HEAD_DIM
    cos = jnp.concatenate([jnp.ones((t_ctx, LANES), F32), jnp.tile(cos, (1, reps))], axis=0)
    sin = jnp.concatenate([jnp.zeros((t_ctx, LANES), F32), jnp.tile(sin, (1, reps))], axis=0)
    return jnp.tile(cos, (batch, 1)), jnp.tile(sin, (batch, 1))


def kernel(x, c, ctx, c_ctx, ada_w, ada_b, norm_gain, w_in, conv_w, rw_w0, rw_w_up, rw_a0, rw_a_up,
           rw_g_up, rw_k_k, rw_k_a, rw_r_k, rw_gn_gain, rw_gn_bias, da_q_gain, da_k_gain, da_lambda,
           da_sub_gain, w_out, peer_wq, peer_keys, peer_u, peer_v):
    batch, t_lat, d = x.shape
    t_ctx = ctx.shape[1]
    seq = t_ctx + t_lat
    n = batch * seq
    depth = ada_w.shape[0]
    rw = rw_k_k.shape[1]
    lora_w = rw_w_up.shape[2] + rw_a_up.shape[2] + rw_g_up.shape[1]
    da_w = w_out.shape[1] - rw
    p_heads, _, nk, half = peer_keys.shape[1:]
    assert 2 * batch * (rw // HEAD_DIM) == LANES, "the recurrence kernel fills the lanes with (half, batch, head)"

    tm = _tile(math.gcd(t_lat, t_ctx), 256)
    tile_pos = (jnp.arange(n // tm, dtype=jnp.int32) * tm) % seq
    gid = jnp.where(tile_pos < t_ctx, batch, jnp.arange(n // tm, dtype=jnp.int32) * tm // seq)
    first = ((tile_pos == 0) | (tile_pos == t_ctx)).astype(jnp.int32)
    last = ((tile_pos + tm == t_ctx) | (tile_pos + tm == seq)).astype(jnp.int32)

    xs = jnp.concatenate([ctx, x], axis=1).reshape(n, d)
    c8 = jnp.concatenate([c, c_ctx[None], jnp.zeros((8 - batch - 1, d), F32)], axis=0)
    mods = _modulation(c8, ada_w, ada_b)[:, :batch + 1].reshape(depth, batch + 1, 6, 1, d)
    cos_t, sin_t = _rope_tables(t_lat, t_ctx, batch)

    cols = jnp.concatenate([jnp.arange(3 * rw), jnp.arange(3 * rw + lora_w, w_in.shape[2]),
                            jnp.arange(3 * rw, 3 * rw + lora_w)])
    tn_in = _tile(w_in.shape[2], 1280, LANES)
    ts = _tile(tm, 128)

    for l in range(depth):
        lam_init = 0.8 - 0.6 * math.exp(-0.3 * l)
        mod = [mods[l, :, j] for j in range(6)]

        p = _norm_proj_wide(xs, gid, norm_gain[l, 0], mod[0], mod[1], w_in[l][:, cols].astype(BF16), tm,
                            _tile(n, 1024), tn_in)
        q8, v, bonus, gate = _rwkv_prep(
            p, first, last, conv_w[l], rw_w0[l], rw_w_up[l], rw_a0[l], rw_a_up[l], rw_g_up[l],
            rw_k_k[l], rw_k_a[l], rw_r_k[l].reshape(-1), tm)
        qs, vs = _to_streams(q8, v, batch, ts)
        yf, yb = _rwkv_scan(qs, vs, t_ctx, tc=_tile(math.gcd(t_lat, t_ctx), 32))
        rw_out = _rwkv_finish(yf, yb, bonus, gate, rw_gn_gain[l], rw_gn_bias[l], batch, ts)

        qkv = _da_prep(p, cos_t, sin_t, da_q_gain[l], da_k_gain[l], 3 * rw, da_w, tm)
        da_out = _diff_attention(qkv, da_lambda[l], da_sub_gain[l], lam_init, batch, t_ctx, tm)
        xs = _gated_proj(rw_out, da_out, w_out[l].astype(BF16), xs, gid, mod[2], tm)

        h, scores = _peer_query(xs, gid, norm_gain[l, 1], mod[3], mod[4], peer_wq[l].astype(BF16),
                                peer_keys[l].reshape(2 * p_heads, nk, half), tm)
        e1, bi, e2, r2 = _peer_select(scores, p_heads, _tile(n, 512, LANES))
        out = _peer_dense(h, peer_u[l].astype(BF16), peer_v[l].astype(BF16), e1, bi, e2, r2,
                          tl=_tile(n, 512, LANES), et=4 * nk, chunk=2 * nk)
        xs = _gated_add(xs, out, gid, mod[5], tm)

    return xs.reshape(batch, seq, d)[:, t_ctx:]
```

```python
import functools
import math

import jax
import jax.numpy as jnp
from jax import lax
from jax.experimental import pallas as pl
from jax.experimental.pallas import tpu as pltpu

F32 = jnp.float32
BF16 = jnp.bfloat16

HEAD_DIM = 64
HEAD_SHIFT = HEAD_DIM.bit_length() - 1
ROPE_SHIFT = (HEAD_DIM // 4).bit_length() - 1
LANES = 128
GRID_W = 64
ROPE_THETA = 10000.0
NORM_EPS = 1e-6
GN_EPS = 64e-5
L2_EPS = 1e-12
PEER_TOPK = 16
LOG2_E = 1.4426950408889634
VMEM_LIMIT = 56 * 1024 * 1024


def _cparams(sem):
    return pltpu.CompilerParams(dimension_semantics=sem, vmem_limit_bytes=VMEM_LIMIT)


def _mm(a, b):
    return jnp.dot(a.astype(BF16), b.astype(BF16), preferred_element_type=F32)


def _segment_ones():
    r = lax.broadcasted_iota(jnp.int32, (LANES, LANES), 0) >> HEAD_SHIFT
    c = lax.broadcasted_iota(jnp.int32, (LANES, LANES), 1) >> HEAD_SHIFT
    return jnp.where(r == c, 1.0, 0.0).astype(BF16)


def _segsum(x, ones_bd):
    hi = x.astype(BF16)
    lo = (x - hi.astype(F32)).astype(BF16)
    return (jnp.dot(hi, ones_bd, preferred_element_type=F32)
            + jnp.dot(lo, ones_bd, preferred_element_type=F32))


def _tile(n, pref, align=8):
    for t in range(min(n, pref) // align * align, 0, -align):
        if n % t == 0:
            return t
    return n


def _mod_kernel(c_ref, w_ref, b_ref, o_ref):
    cv = c_ref[...]
    act = cv * jax.nn.sigmoid(cv)
    o_ref[...] = _mm(act, w_ref[...]) + b_ref[...]


def _modulation(c8, ada_w, ada_b):
    depth, d, n6 = ada_w.shape
    tn = _tile(n6, 1536)
    return pl.pallas_call(
        _mod_kernel,
        grid=(depth, n6 // tn),
        in_specs=[
            pl.BlockSpec((8, d), lambda l, j: (0, 0)),
            pl.BlockSpec((None, d, tn), lambda l, j: (l, 0, j)),
            pl.BlockSpec((None, 1, tn), lambda l, j: (l, 0, j)),
        ],
        out_specs=pl.BlockSpec((None, 8, tn), lambda l, j: (l, 0, j)),
        out_shape=jax.ShapeDtypeStruct((depth, 8, n6), F32),
        compiler_params=_cparams(("arbitrary", "arbitrary")),
        name="modulation",
    )(c8, ada_w, ada_b.reshape(depth, 1, n6))


def _norm_mod_kernel(gid_ref, x_ref, gain_ref, shift_ref, scale_ref, o_ref):
    del gid_ref
    x = x_ref[...]
    ms = jnp.mean(x * x, axis=-1, keepdims=True)
    y = x * lax.rsqrt(ms + NORM_EPS) * gain_ref[...]
    o_ref[...] = (y * (1.0 + scale_ref[...]) + shift_ref[...]).astype(BF16)


def _matmul_kernel(h_ref, w_ref, o_ref):
    o_ref[...] = jnp.dot(h_ref[...], w_ref[...], preferred_element_type=F32)


def _norm_proj_wide(x, gid, gain, shift, scale, w, tm, tm_mm, tn):
    n, d = x.shape
    nout = w.shape[1]
    h = pl.pallas_call(
        _norm_mod_kernel,
        grid_spec=pltpu.PrefetchScalarGridSpec(
            num_scalar_prefetch=1,
            grid=(n // tm,),
            in_specs=[
                pl.BlockSpec((tm, d), lambda i, g: (i, 0)),
                pl.BlockSpec((1, d), lambda i, g: (0, 0)),
                pl.BlockSpec((None, 1, d), lambda i, g: (g[i], 0, 0)),
                pl.BlockSpec((None, 1, d), lambda i, g: (g[i], 0, 0)),
            ],
            out_specs=pl.BlockSpec((tm, d), lambda i, g: (i, 0)),
        ),
        out_shape=jax.ShapeDtypeStruct((n, d), BF16),
        compiler_params=_cparams(("arbitrary",)),
        name="norm_mod",
    )(gid, x, gain.reshape(1, d), shift, scale)
    return pl.pallas_call(
        _matmul_kernel,
        grid=(nout // tn, n // tm_mm),
        in_specs=[
            pl.BlockSpec((tm_mm, d), lambda j, i: (i, 0)),
            pl.BlockSpec((d, tn), lambda j, i: (0, j)),
        ],
        out_specs=pl.BlockSpec((tm_mm, tn), lambda j, i: (i, j)),
        out_shape=jax.ShapeDtypeStruct((n, nout), F32),
        compiler_params=_cparams(("arbitrary", "arbitrary")),
        name="proj_matmul",
    )(h, w)


def _gated_proj_kernel(gid_ref, y1_ref, y2_ref, w1_ref, w2_ref, x_ref, gate_ref, o_ref):
    del gid_ref
    mix = (jnp.dot(y1_ref[...], w1_ref[...], preferred_element_type=F32)
           + jnp.dot(y2_ref[...], w2_ref[...], preferred_element_type=F32))
    o_ref[...] = x_ref[...] + gate_ref[...] * mix


def _gated_proj(y1, y2, w, x, gid, gate, tm):
    n, k1 = y1.shape
    k2 = y2.shape[1]
    d = w.shape[1]
    return pl.pallas_call(
        _gated_proj_kernel,
        grid_spec=pltpu.PrefetchScalarGridSpec(
            num_scalar_prefetch=1,
            grid=(n // tm,),
            in_specs=[
                pl.BlockSpec((tm, k1), lambda i, g: (i, 0)),
                pl.BlockSpec((tm, k2), lambda i, g: (i, 0)),
                pl.BlockSpec((k1, d), lambda i, g: (0, 0)),
                pl.BlockSpec((k2, d), lambda i, g: (k1 // k2, 0)),
                pl.BlockSpec((tm, d), lambda i, g: (i, 0)),
                pl.BlockSpec((None, 1, d), lambda i, g: (g[i], 0, 0)),
            ],
            out_specs=pl.BlockSpec((tm, d), lambda i, g: (i, 0)),
        ),
        out_shape=jax.ShapeDtypeStruct((n, d), F32),
        compiler_params=_cparams(("arbitrary",)),
        name="gated_proj",
    )(gid, y1, y2, w, w, x, gate)


def _rwkv_prep_kernel(first_ref, last_ref, x_ref, prev_ref, next_ref, tail_ref, cw_ref,
                      w0_ref, wup_ref, a0_ref, aup_ref, gup_ref, kk_ref, ka_ref, rk_ref,
                      q_out, v_out, bonus_out, gate_out, *, rw, lw, la):
    i = pl.program_id(0)
    tm = x_ref.shape[0]
    x = x_ref[...]
    row = lax.broadcasted_iota(jnp.int32, x.shape, 0)
    keep_prev = jnp.where(first_ref[i] == 1, 0.0, 1.0)
    keep_next = jnp.where(last_ref[i] == 1, 0.0, 1.0)
    halo_prev = prev_ref[7:8, :] * keep_prev
    halo_next = next_ref[0:1, :] * keep_next
    x_prev = jnp.where(row == 0, halo_prev, pltpu.roll(x, 1, axis=0))
    x_next = jnp.where(row == tm - 1, halo_next, pltpu.roll(x, tm - 1, axis=0))
    conv = x_prev * cw_ref[0:1, :] + x * cw_ref[1:2, :] + x_next * cw_ref[2:3, :]
    r = conv[:, 0:rw]
    k = conv[:, rw:2 * rw]
    v = conv[:, 2 * rw:3 * rw]
    tail = tail_ref[...]
    w_lo = jnp.tanh(tail[:, 0:lw])
    a_lo = tail[:, lw:lw + la]
    g_lo = jax.nn.sigmoid(tail[:, lw + la:])
    ones_bd = _segment_ones()

    def seg(t):
        return jnp.concatenate(
            [_segsum(t[:, c * LANES:(c + 1) * LANES], ones_bd) for c in range(rw // LANES)], axis=1)

    kap = k * kk_ref[...]
    nrm = jnp.sqrt(seg(kap * kap))
    kap = kap / jnp.maximum(nrm, L2_EPS)
    q_out[0] = r
    q_out[1] = kap
    v_out[...] = v
    gate_out[...] = _mm(g_lo, gup_ref[...])
    bonus = jnp.zeros_like(r)
    for d in range(2):
        z = w0_ref[d:d + 1, :] + _mm(w_lo, wup_ref[d])
        log_w = -jax.nn.softplus(-z) - 0.5
        q_out[2 + d] = jnp.exp(-jnp.exp(log_w))
        a = jax.nn.sigmoid(a0_ref[d:d + 1, :] + _mm(a_lo, aup_ref[d]))
        k_mod = k * (1.0 + (a - 1.0) * ka_ref[...])
        q_out[4 + d] = k_mod
        q_out[6 + d] = kap * a
        bonus = bonus + seg(r * k_mod * rk_ref[...]) * v
    bonus_out[...] = bonus


def _rwkv_prep(p, first, last, conv_w, w0, w_up, a0, a_up, g_up, k_k, k_a, r_k, tm):
    n = p.shape[0]
    rw = k_k.shape[0]
    lw, la, lg = w_up.shape[1], a_up.shape[1], g_up.shape[0]
    tail_w = lw + la + lg
    tail_blk = (p.shape[1] - tail_w) // tail_w
    nb8 = n // 8
    row2 = lambda i, f, l: (0, 0)
    row3 = lambda i, f, l: (0, 0, 0)
    one = jax.ShapeDtypeStruct((n, rw), F32)
    one_spec = pl.BlockSpec((tm, rw), lambda i, f, l: (i, 0))
    return pl.pallas_call(
        functools.partial(_rwkv_prep_kernel, rw=rw, lw=lw, la=la),
        grid_spec=pltpu.PrefetchScalarGridSpec(
            num_scalar_prefetch=2,
            grid=(n // tm,),
            in_specs=[
                pl.BlockSpec((tm, 3 * rw), lambda i, f, l: (i, 0)),
                pl.BlockSpec((8, 3 * rw), lambda i, f, l: (jnp.maximum(i * (tm // 8) - 1, 0), 0)),
                pl.BlockSpec((8, 3 * rw), lambda i, f, l: (jnp.minimum((i + 1) * (tm // 8), nb8 - 1), 0)),
                pl.BlockSpec((tm, tail_w), lambda i, f, l: (i, tail_blk)),
                pl.BlockSpec((3, 3 * rw), row2),
                pl.BlockSpec((2, rw), row2),
                pl.BlockSpec((2, lw, rw), row3),
                pl.BlockSpec((2, rw), row2),
                pl.BlockSpec((2, la, rw), row3),
                pl.BlockSpec((lg, rw), row2),
                pl.BlockSpec((1, rw), row2),
                pl.BlockSpec((1, rw), row2),
                pl.BlockSpec((1, rw), row2),
            ],
            out_specs=[pl.BlockSpec((8, tm, rw), lambda i, f, l: (0, i, 0)), one_spec, one_spec, one_spec],
        ),
        out_shape=[jax.ShapeDtypeStruct((8, n, rw), F32), one, one, one],
        compiler_params=_cparams(("arbitrary",)),
        name="rwkv_prep",
    )(first, last, p, p, p, p, conv_w, w0, w_up, a0, a_up, g_up,
      k_k.reshape(1, rw), k_a.reshape(1, rw), r_k.reshape(1, rw))


def _to_streams_kernel(x_ref, o_ref, xt_ref, *, split_value):
    batch, tm, rw = x_ref.shape
    heads = rw // HEAD_DIM
    for b in range(batch):
        xt_ref[b] = x_ref[b].T
    rows = HEAD_DIM // 2 if split_value else HEAD_DIM
    for j in range(rows):
        second = j + HEAD_DIM // 2 if split_value else j
        parts = [xt_ref[b, pl.ds(j, heads, stride=HEAD_DIM), :] for b in range(batch)]
        parts += [xt_ref[b, pl.ds(second, heads, stride=HEAD_DIM), :] for b in range(batch)]
        slab = jnp.concatenate(parts, axis=0).T
        if split_value:
            o_ref[:, j, :] = slab
        else:
            o_ref[j] = slab


def _to_streams(q, v, batch, tm):
    nq, n, rw = q.shape
    seq = n // batch
    scratch = [pltpu.VMEM((batch, rw, tm), F32)]
    qs = pl.pallas_call(
        functools.partial(_to_streams_kernel, split_value=False),
        grid=(nq, seq // tm),
        in_specs=[pl.BlockSpec((None, batch, tm, rw), lambda a, i: (a, 0, i, 0))],
        out_specs=pl.BlockSpec((None, HEAD_DIM, tm, LANES), lambda a, i: (a, 0, i, 0)),
        out_shape=jax.ShapeDtypeStruct((nq, HEAD_DIM, seq, LANES), F32),
        scratch_shapes=scratch,
        compiler_params=_cparams(("arbitrary", "arbitrary")),
        name="to_streams",
    )(q.reshape(nq, batch, seq, rw))
    vs = pl.pallas_call(
        functools.partial(_to_streams_kernel, split_value=True),
        grid=(seq // tm,),
        in_specs=[pl.BlockSpec((batch, tm, rw), lambda i: (0, i, 0))],
        out_specs=pl.BlockSpec((tm, HEAD_DIM // 2, LANES), lambda i: (i, 0, 0)),
        out_shape=jax.ShapeDtypeStruct((seq, HEAD_DIM // 2, LANES), F32),
        scratch_shapes=scratch,
        compiler_params=_cparams(("arbitrary",)),
        name="to_streams_v",
    )(v.reshape(batch, seq, rw))
    return qs, vs


def _rwkv_scan_kernel(rf, kapf, wf, kf, bf, vf, rb, kapb, wb, kb, bb, vb, yf, yb, s_ref):
    hd, tc, ns = rf.shape
    half = vf.shape[1]

    @pl.when(pl.program_id(0) == 0)
    def _():
        s_ref[...] = jnp.zeros_like(s_ref)

    dirs = ((rf, kapf, wf, kf, bf, vf, yf), (rb, kapb, wb, kb, bb, vb, yb))

    def state_times_kappa(d, t):
        acc = jnp.zeros((half, ns), F32)
        for kk in range(hd):
            acc = acc + s_ref[d, kk] * dirs[d][1][kk, pl.ds(t, 1), :]
        return acc

    def update(d, t, t_next, sa):
        r_ref, kap_ref, w_ref, k_ref, b_ref, v_ref, y_ref = dirs[d]
        vv = v_ref[t]
        y = jnp.zeros((half, ns), F32)
        sa_next = jnp.zeros((half, ns), F32)
        for kk in range(hd):
            s_new = (s_ref[d, kk] * w_ref[kk, pl.ds(t, 1), :]
                     + (vv * k_ref[kk, pl.ds(t, 1), :] - sa * b_ref[kk, pl.ds(t, 1), :]))
            s_ref[d, kk] = s_new
            y = y + s_new * r_ref[kk, pl.ds(t, 1), :]
            sa_next = sa_next + s_new * kap_ref[kk, pl.ds(t_next, 1), :]
        y_ref[t] = y
        return sa_next

    def step(j, carry):
        sa_f, sa_b = carry
        sa_f = update(0, j, jnp.minimum(j + 1, tc - 1), sa_f)
        tb = tc - 1 - j
        sa_b = update(1, tb, jnp.maximum(tb - 1, 0), sa_b)
        return sa_f, sa_b

    lax.fori_loop(0, tc, step, (state_times_kappa(0, 0), state_times_kappa(1, tc - 1)))


def _rwkv_scan(qs, vs, t_ctx, tc):
    _, hd, seq, ns = qs.shape
    half = vs.shape[1]
    nb = seq // tc
    nbc = t_ctx // tc

    def back_block(i):
        return jnp.where(i < nbc, nbc - 1 - i, nb - 1 - (i - nbc))

    def qspec(a, back):
        if back:
            return pl.BlockSpec((None, hd, tc, ns), lambda i: (a, 0, back_block(i), 0))
        return pl.BlockSpec((None, hd, tc, ns), lambda i: (a, 0, i, 0))

    def vspec(back):
        if back:
            return pl.BlockSpec((tc, half, ns), lambda i: (back_block(i), 0, 0))
        return pl.BlockSpec((tc, half, ns), lambda i: (i, 0, 0))

    in_specs = ([qspec(a, False) for a in (0, 1, 2, 4, 6)] + [vspec(False)]
                + [qspec(a, True) for a in (0, 1, 3, 5, 7)] + [vspec(True)])
    out = jax.ShapeDtypeStruct((seq, half, ns), F32)
    return pl.pallas_call(
        _rwkv_scan_kernel,
        grid=(nb,),
        in_specs=in_specs,
        out_specs=[vspec(False), vspec(True)],
        out_shape=[out, out],
        scratch_shapes=[pltpu.VMEM((2, hd, half, ns), F32)],
        compiler_params=_cparams(("arbitrary",)),
        name="rwkv_scan",
    )(*([qs] * 5 + [vs] + [qs] * 5 + [vs]))


def _rwkv_finish_kernel(yf_ref, yb_ref, bonus_ref, gate_ref, gain_ref, bias_ref, o_ref, xt_ref):
    batch, tm, rw = bonus_ref.shape
    heads = rw // HEAD_DIM
    half = yf_ref.shape[1]
    for j in range(half):
        slab = (yf_ref[:, j, :] + yb_ref[:, j, :]).T
        for g in range(2 * batch):
            xt_ref[g % batch, pl.ds(j + half * (g // batch), heads, stride=HEAD_DIM), :] = (
                slab[g * heads:(g + 1) * heads])
    ones_bd = _segment_ones()
    for b in range(batch):
        y = xt_ref[b].T
        outs = []
        for c in range(rw // LANES):
            yc = y[:, c * LANES:(c + 1) * LANES]
            mu = _segsum(yc, ones_bd) * (1.0 / HEAD_DIM)
            dev = yc - mu
            var = _segsum(dev * dev, ones_bd) * (1.0 / HEAD_DIM)
            outs.append(dev * lax.rsqrt(var + GN_EPS))
        yn = jnp.concatenate(outs, axis=1)
        o_ref[b] = ((yn * gain_ref[...] + bias_ref[...] + bonus_ref[b]) * gate_ref[b]).astype(o_ref.dtype)


def _rwkv_finish(yf, yb, bonus, gate, gain, bias, batch, tm):
    seq, half, ns = yf.shape
    n, rw = bonus.shape
    yspec = pl.BlockSpec((tm, half, ns), lambda i: (i, 0, 0))
    spec = pl.BlockSpec((batch, tm, rw), lambda i: (0, i, 0))
    vec = pl.BlockSpec((1, rw), lambda i: (0, 0))
    out = pl.pallas_call(
        _rwkv_finish_kernel,
        grid=(seq // tm,),
        in_specs=[yspec, yspec, spec, spec, vec, vec],
        out_specs=spec,
        out_shape=jax.ShapeDtypeStruct((batch, seq, rw), BF16),
        scratch_shapes=[pltpu.VMEM((batch, rw, tm), F32)],
        compiler_params=_cparams(("arbitrary",)),
        name="rwkv_finish",
    )(yf, yb, bonus.reshape(batch, seq, rw), gate.reshape(batch, seq, rw),
      gain.reshape(1, rw), bias.reshape(1, rw))
    return out.reshape(n, rw)


def _da_prep_kernel(x_ref, cos_ref, sin_ref, qg_ref, kg_ref, o_ref):
    s = pl.program_id(1)

    @pl.when(s == 2)
    def _():
        o_ref[...] = x_ref[...].astype(BF16)

    @pl.when(s < 2)
    def _():
        ones_bd = _segment_ones()
        gain = jnp.where(s == 0, qg_ref[...] * (HEAD_DIM ** -0.5 * LOG2_E), kg_ref[...])
        cos = cos_ref[...]
        sin = sin_ref[...]
        lane = lax.broadcasted_iota(jnp.int32, cos.shape, 1)
        even = ((lane >> ROPE_SHIFT) & 1) == 0
        x = x_ref[...]
        for c in range(x.shape[1] // LANES):
            xc = x[:, c * LANES:(c + 1) * LANES]
            ms = _segsum(xc * xc, ones_bd) * (1.0 / HEAD_DIM)
            xn = xc * lax.rsqrt(ms + NORM_EPS) * gain
            swapped = jnp.where(even, pltpu.roll(xn, LANES - HEAD_DIM // 4, axis=1),
                                pltpu.roll(xn, HEAD_DIM // 4, axis=1))
            o_ref[:, c * LANES:(c + 1) * LANES] = (xn * cos + swapped * sin).astype(BF16)


def _da_prep(p, cos_t, sin_t, q_gain, k_gain, col0, width, tm):
    n = p.shape[0]
    blk0 = col0 // width
    qg = jnp.tile(q_gain, LANES // HEAD_DIM).reshape(1, LANES)
    kg = jnp.tile(k_gain, LANES // HEAD_DIM).reshape(1, LANES)
    return pl.pallas_call(
        _da_prep_kernel,
        grid=(n // tm, 3),
        in_specs=[
            pl.BlockSpec((tm, width), lambda i, s: (i, blk0 + s)),
            pl.BlockSpec((tm, LANES), lambda i, s: (i, 0)),
            pl.BlockSpec((tm, LANES), lambda i, s: (i, 0)),
            pl.BlockSpec((1, LANES), lambda i, s: (0, 0)),
            pl.BlockSpec((1, LANES), lambda i, s: (0, 0)),
        ],
        out_specs=pl.BlockSpec((None, tm, width), lambda i, s: (s, i, 0)),
        out_shape=jax.ShapeDtypeStruct((3, n, width), BF16),
        compiler_params=_cparams(("arbitrary", "arbitrary")),
        name="da_prep",
    )(p, cos_t, sin_t, qg, kg)


def _diff_attn_kernel(q_ref, k_ref, v_ref, lam_ref, sg_ref, o_ref, *, lam_init, t_ctx):
    tq = q_ref.shape[0]
    lv = lam_ref[...]
    lam = (jnp.exp(jnp.sum(lv[0:1] * lv[1:2], axis=1, keepdims=True))
           - jnp.exp(jnp.sum(lv[2:3] * lv[3:4], axis=1, keepdims=True)) + lam_init)

    def attend(keys, vals):
        q = q_ref[...]
        lane = lax.broadcasted_iota(jnp.int32, q.shape, 1)
        zero = jnp.zeros_like(q)
        outs = []
        for m in range(2):
            qm = jnp.where((lane >> HEAD_SHIFT) == m, q, zero)
            s = lax.dot_general(qm, keys, (((1,), (1,)), ((), ())), preferred_element_type=F32)
            p = jnp.exp2(s - jnp.max(s, axis=1, keepdims=True))
            den = jnp.sum(p, axis=1, keepdims=True)
            outs.append(jnp.dot(p.astype(BF16), vals, preferred_element_type=F32) / den)
        o = outs[0] - lam * outs[1]
        ms = jnp.mean(o * o, axis=1, keepdims=True)
        o_ref[...] = (o * lax.rsqrt(ms + NORM_EPS) * sg_ref[...] * (1.0 - lam_init)).astype(o_ref.dtype)

    is_ctx = pl.program_id(2) < t_ctx // tq

    @pl.when(is_ctx)
    def _():
        attend(k_ref[0:t_ctx, :], v_ref[0:t_ctx, :])

    @pl.when(jnp.logical_not(is_ctx))
    def _():
        attend(k_ref[...], v_ref[...])


def _diff_attention(qkv, lam_vec, sub_gain, lam_init, batch, t_ctx, tq):
    _, n, width = qkv.shape
    heads = width // LANES
    seq = n // batch
    nq = seq // tq
    return pl.pallas_call(
        functools.partial(_diff_attn_kernel, lam_init=lam_init, t_ctx=t_ctx),
        grid=(batch, heads, nq),
        in_specs=[
            pl.BlockSpec((None, tq, LANES), lambda b, h, i: (0, b * nq + i, h)),
            pl.BlockSpec((None, seq, LANES), lambda b, h, i: (1, b, h)),
            pl.BlockSpec((None, seq, LANES), lambda b, h, i: (2, b, h)),
            pl.BlockSpec((4, HEAD_DIM), lambda b, h, i: (0, 0)),
            pl.BlockSpec((1, LANES), lambda b, h, i: (0, 0)),
        ],
        out_specs=pl.BlockSpec((tq, LANES), lambda b, h, i: (b * nq + i, h)),
        out_shape=jax.ShapeDtypeStruct((n, width), BF16),
        compiler_params=_cparams(("arbitrary", "arbitrary", "arbitrary")),
        name="diff_attn",
    )(qkv, qkv, qkv, lam_vec, sub_gain.reshape(1, LANES))


def _peer_query_kernel(gid_ref, x_ref, gain_ref, shift_ref, scale_ref, w_ref, keys_ref, h_ref, s_ref):
    del gid_ref
    x = x_ref[...]
    ms = jnp.mean(x * x, axis=-1, keepdims=True)
    y = x * lax.rsqrt(ms + NORM_EPS) * gain_ref[...]
    h = (y * (1.0 + scale_ref[...]) + shift_ref[...]).astype(BF16)
    h_ref[...] = h
    q = jnp.dot(h, w_ref[...], preferred_element_type=F32)
    half = keys_ref.shape[2]
    for j in range(keys_ref.shape[0]):
        s_ref[j] = lax.dot_general(keys_ref[j], q[:, j * half:(j + 1) * half], (((1,), (1,)), ((), ())),
                                   precision=lax.Precision.HIGHEST, preferred_element_type=F32)


def _peer_query(x, gid, gain, shift, scale, w, keys, tm):
    n, d = x.shape
    hp, nk, half = keys.shape
    return pl.pallas_call(
        _peer_query_kernel,
        grid_spec=pltpu.PrefetchScalarGridSpec(
            num_scalar_prefetch=1,
            grid=(n // tm,),
            in_specs=[
                pl.BlockSpec((tm, d), lambda i, g: (i, 0)),
                pl.BlockSpec((1, d), lambda i, g: (0, 0)),
                pl.BlockSpec((None, 1, d), lambda i, g: (g[i], 0, 0)),
                pl.BlockSpec((None, 1, d), lambda i, g: (g[i], 0, 0)),
                pl.BlockSpec((d, hp * half), lambda i, g: (0, 0)),
                pl.BlockSpec((hp, nk, half), lambda i, g: (0, 0, 0)),
            ],
            out_specs=[pl.BlockSpec((tm, d), lambda i, g: (i, 0)),
                       pl.BlockSpec((hp, nk, tm), lambda i, g: (0, 0, i))],
        ),
        out_shape=[jax.ShapeDtypeStruct((n, d), BF16), jax.ShapeDtypeStruct((hp, nk, n), F32)],
        compiler_params=_cparams(("arbitrary",)),
        name="peer_query",
    )(gid, x, gain.reshape(1, d), shift, scale, w, keys)


SPLIT_B = 4


def _peer_select_kernel(s_ref, e1_ref, bi_ref, e2_ref, r2_ref, rank_ref, val_ref, sel_ref):
    nk, tl = s_ref.shape[1:]
    k = PEER_TOPK
    neg = -jnp.inf
    unranked = float(2 * k)

    def top(s, exact):
        key = lax.broadcasted_iota(jnp.int32, s.shape, 0)
        work = s
        rank = jnp.full(s.shape, unranked, F32)
        vals = []
        for r in range(k):
            m = jnp.max(work, axis=0, keepdims=True)
            hit = work == m
            if exact:
                hit = key == jnp.min(jnp.where(hit, key, nk), axis=0, keepdims=True)
            work = jnp.where(hit, neg, work)
            rank = jnp.where(hit, float(r), rank)
            vals.append(m)
        removed = jnp.sum(jnp.where(rank < unranked, 1.0, 0.0), axis=0, keepdims=True)
        return jnp.concatenate(vals, axis=0), rank, removed

    def stage1(exact):
        removed = jnp.zeros((1, tl), F32)
        for p in range(2):
            vals, rank, rem = top(s_ref[p], exact)
            val_ref[p] = vals
            rank_ref[p] = rank
            removed = jnp.maximum(removed, rem)
        return jnp.max(removed)

    sub = lax.broadcasted_iota(jnp.int32, (k, tl), 0)
    n_a = k // (SPLIT_B + 1)

    def stage2(exact):
        v1 = val_ref[0]
        v2 = val_ref[1]
        work, flat, valid = [], [], []
        for b in range(SPLIT_B):
            ok = sub <= k // (b + 1) - 1
            work.append(jnp.where(ok, v1 + v2[b:b + 1], neg))
            flat.append(jnp.where(ok, sub * k + b, -1))
            valid.append(ok)
        for a in range(n_a):
            ok = (sub >= SPLIT_B) & (sub <= k // (a + 1) - 1)
            work.append(jnp.where(ok, v1[a:a + 1] + v2, neg))
            flat.append(jnp.where(ok, sub + a * k, -1))
            valid.append(ok)
        top_sum = v1[0:1] + v2[0:1]
        z = jnp.zeros((1, tl), F32)
        for _ in range(k):
            m = jnp.max(functools.reduce(jnp.maximum, work), axis=0, keepdims=True)
            z = z + jnp.exp(m - top_sum)
            if exact:
                cand = functools.reduce(jnp.minimum, [jnp.where(w == m, f, k * k) for w, f in zip(work, flat)])
                idx = jnp.min(cand, axis=0, keepdims=True)
            for n in range(len(work)):
                hit = (flat[n] == idx) if exact else (work[n] == m)
                work[n] = jnp.where(hit, neg, work[n])
        count = [jnp.where(ok & (w == neg), 1.0, 0.0) for ok, w in zip(valid, work)]
        per_a = functools.reduce(jnp.add, count[:SPLIT_B])
        for a in range(n_a):
            extra = jnp.sum(count[SPLIT_B + a], axis=0, keepdims=True)
            per_a = per_a + jnp.where(sub == a, extra, 0.0)
        sel_ref[0:k, :] = per_a
        sel_ref[k:k + 1, :] = z
        return jnp.max(jnp.sum(per_a, axis=0, keepdims=True))

    @pl.when(stage1(False) > float(k))
    def _():
        stage1(True)

    @pl.when(stage2(False) > float(k))
    def _():
        stage2(True)

    s1 = s_ref[0]
    rank1 = rank_ref[0]
    bi = jnp.zeros(s1.shape, F32)
    for a in range(k):
        bi = jnp.where(rank1 == float(a), sel_ref[a:a + 1, :], bi)
    bi_ref[...] = bi
    e1_ref[...] = 0.5 * jnp.exp(s1 - val_ref[0, 0:1, :]) / sel_ref[k:k + 1, :]
    e2_ref[...] = jnp.exp(s_ref[1] - val_ref[1, 0:1, :]).astype(e2_ref.dtype)
    r2_ref[...] = rank_ref[1].astype(r2_ref.dtype)


def _peer_select(scores, heads, tl):
    hp, nk, n = scores.shape
    k = PEER_TOPK
    out = jax.ShapeDtypeStruct((heads, nk, n), F32)
    spec = pl.BlockSpec((None, nk, tl), lambda t, h: (h, 0, t))
    return pl.pallas_call(
        _peer_select_kernel,
        grid=(n // tl, heads),
        in_specs=[pl.BlockSpec((None, 2, nk, tl), lambda t, h: (h, 0, 0, t))],
        out_specs=[spec] * 4,
        out_shape=[out] * 4,
        scratch_shapes=[pltpu.VMEM((2, nk, tl), F32), pltpu.VMEM((2, k, tl), F32),
                        pltpu.VMEM((k + 8, tl), F32)],
        compiler_params=_cparams(("arbitrary", "arbitrary")),
        name="peer_select",
    )(scores.reshape(heads, 2, nk, n))


def _peer_dense_kernel(gid_ref, h_ref, u_ref, v_ref, e1_ref, bi_ref, e2_ref, r2_ref, x_ref, *rest, nk, chunk):
    del gid_ref
    gate_refs, o_ref = rest[:-1], rest[-1]
    e = pl.program_id(1)
    heads = e1_ref.shape[0]
    et = u_ref.shape[0]

    @pl.when(e == 0)
    def _():
        o_ref[...] = jnp.zeros_like(o_ref)

    total = None
    for c in range(et // chunk):
        a = lax.dot_general(u_ref[c * chunk:(c + 1) * chunk, :], h_ref[...], (((1,), (1,)), ((), ())),
                            preferred_element_type=F32)
        act = a * (1.0 + lax.erf(a * (2.0 ** -0.5)))
        parts = []
        for ii in range(chunk // nk):
            i = (e * et + c * chunk) // nk + ii
            g = jnp.zeros((nk, a.shape[1]), F32)
            for hh in range(heads):
                e1 = e1_ref[hh, pl.ds(i, 1), :]
                bi = bi_ref[hh, pl.ds(i, 1), :]
                g = g + jnp.where(r2_ref[hh] < bi, e2_ref[hh] * e1, 0.0)
            parts.append(g * act[ii * nk:(ii + 1) * nk])
        m = jnp.concatenate(parts, axis=0) if len(parts) > 1 else parts[0]
        contrib = jnp.dot(m.T.astype(BF16), v_ref[c * chunk:(c + 1) * chunk, :], preferred_element_type=F32)
        total = contrib if total is None else total + contrib
    o_ref[...] += total

    @pl.when(e == pl.num_programs(1) - 1)
    def _():
        tm = o_ref.shape[0] // len(gate_refs)
        for part, gate_ref in enumerate(gate_refs):
            rows = slice(part * tm, (part + 1) * tm)
            o_ref[rows, :] = x_ref[rows, :] + gate_ref[...] * o_ref[rows, :]


def _peer_dense(h, u, v, e1, bi, e2, r2, x, gid, gate, tm, tl, et, chunk):
    n, d = h.shape
    ne = u.shape[0]
    heads, nk, _ = e1.shape
    parts = tl // tm
    fac = pl.BlockSpec((heads, nk, tl), lambda t, e, g: (0, 0, t))
    gates = [pl.BlockSpec((None, 1, d), lambda t, e, g, part=part: (g[t * parts + part], 0, 0))
             for part in range(parts)]
    return pl.pallas_call(
        functools.partial(_peer_dense_kernel, nk=nk, chunk=chunk),
        grid_spec=pltpu.PrefetchScalarGridSpec(
            num_scalar_prefetch=1,
            grid=(n // tl, ne // et),
            in_specs=[
                pl.BlockSpec((tl, d), lambda t, e, g: (t, 0)),
                pl.BlockSpec((et, d), lambda t, e, g: (e, 0)),
                pl.BlockSpec((et, d), lambda t, e, g: (e, 0)),
                fac, fac, fac, fac,
                pl.BlockSpec((tl, d), lambda t, e, g: (t, 0)),
            ] + gates,
            out_specs=pl.BlockSpec((tl, d), lambda t, e, g: (t, 0)),
        ),
        out_shape=jax.ShapeDtypeStruct((n, d), F32),
        compiler_params=_cparams(("arbitrary", "arbitrary")),
        name="peer_dense",
    )(gid, h, u, v, e1, bi, e2, r2, x, *([gate] * parts))


def _rope_tables(t_lat, t_ctx, batch):
    rows = t_lat // GRID_W
    row = jnp.repeat(jnp.arange(rows, dtype=jnp.int32), GRID_W)
    col = jnp.tile(jnp.arange(GRID_W, dtype=jnp.int32), rows)
    nfreq = HEAD_DIM // 4
    inv_freq = ROPE_THETA ** (-jnp.arange(nfreq, dtype=F32) / nfreq)
    ang_r = row.astype(F32)[:, None] * inv_freq
    ang_c = col.astype(F32)[:, None] * inv_freq
    cos = jnp.concatenate([jnp.cos(ang_r)] * 2 + [jnp.cos(ang_c)] * 2, axis=1)
    sin = jnp.concatenate([-jnp.sin(ang_r), jnp.sin(ang_r), -jnp.sin(ang_c), jnp.sin(ang_c)], axis=1)
    reps = LANES // HEAD_DIM
    cos = jnp.concatenate([jnp.ones((t_ctx, LANES), F32), jnp.tile(cos, (1, reps))], axis=0)
    sin = jnp.concatenate([jnp.zeros((t_ctx, LANES), F32), jnp.tile(sin, (1, reps))], axis=0)
    return jnp.tile(cos, (batch, 1)), jnp.tile(sin, (batch, 1))


def kernel(x, c, ctx, c_ctx, ada_w, ada_b, norm_gain, w_in, conv_w, rw_w0, rw_w_up, rw_a0, rw_a_up,
           rw_g_up, rw_k_k, rw_k_a, rw_r_k, rw_gn_gain, rw_gn_bias, da_q_gain, da_k_gain, da_lambda,
           da_sub_gain, w_out, peer_wq, peer_keys, peer_u, peer_v):
    batch, t_lat, d = x.shape
    t_ctx = ctx.shape[1]
    seq = t_ctx + t_lat
    n = batch * seq
    depth = ada_w.shape[0]
    rw = rw_k_k.shape[1]
    lora_w = rw_w_up.shape[2] + rw_a_up.shape[2] + rw_g_up.shape[1]
    da_w = w_out.shape[1] - rw
    p_heads, _, nk, half = peer_keys.shape[1:]
    assert 2 * batch * (rw // HEAD_DIM) == LANES, "the recurrence kernel fills the lanes with (half, batch, head)"

    tm = _tile(math.gcd(t_lat, t_ctx), 256)
    tile_pos = (jnp.arange(n // tm, dtype=jnp.int32) * tm) % seq
    gid = jnp.where(tile_pos < t_ctx, batch, jnp.arange(n // tm, dtype=jnp.int32) * tm // seq)
    first = ((tile_pos == 0) | (tile_pos == t_ctx)).astype(jnp.int32)
    last = ((tile_pos + tm == t_ctx) | (tile_pos + tm == seq)).astype(jnp.int32)

    xs = jnp.concatenate([ctx, x], axis=1).reshape(n, d)
    c8 = jnp.concatenate([c, c_ctx[None], jnp.zeros((8 - batch - 1, d), F32)], axis=0)
    mods = _modulation(c8, ada_w, ada_b)[:, :batch + 1].reshape(depth, batch + 1, 6, 1, d)
    cos_t, sin_t = _rope_tables(t_lat, t_ctx, batch)

    cols = jnp.concatenate([jnp.arange(3 * rw), jnp.arange(3 * rw + lora_w, w_in.shape[2]),
                            jnp.arange(3 * rw, 3 * rw + lora_w)])
    tn_in = _tile(w_in.shape[2], 1280, LANES)
    ts = _tile(tm, 128)

    for l in range(depth):
        lam_init = 0.8 - 0.6 * math.exp(-0.3 * l)
        mod = [mods[l, :, j] for j in range(6)]

        p = _norm_proj_wide(xs, gid, norm_gain[l, 0], mod[0], mod[1], w_in[l][:, cols].astype(BF16), tm,
                            _tile(n, 1024), tn_in)
        q8, v, bonus, gate = _rwkv_prep(
            p, first, last, conv_w[l], rw_w0[l], rw_w_up[l], rw_a0[l], rw_a_up[l], rw_g_up[l],
            rw_k_k[l], rw_k_a[l], rw_r_k[l].reshape(-1), tm)
        qs, vs = _to_streams(q8, v, batch, ts)
        yf, yb = _rwkv_scan(qs, vs, t_ctx, tc=_tile(math.gcd(t_lat, t_ctx), 32))
        rw_out = _rwkv_finish(yf, yb, bonus, gate, rw_gn_gain[l], rw_gn_bias[l], batch, ts)

        qkv = _da_prep(p, cos_t, sin_t, da_q_gain[l], da_k_gain[l], 3 * rw, da_w, tm)
        da_out = _diff_attention(qkv, da_lambda[l], da_sub_gain[l], lam_init, batch, t_ctx, tm)
        xs = _gated_proj(rw_out, da_out, w_out[l].astype(BF16), xs, gid, mod[2], tm)

        h, scores = _peer_query(xs, gid, norm_gain[l, 1], mod[3], mod[4], peer_wq[l].astype(BF16),
                                peer_keys[l].reshape(2 * p_heads, nk, half), tm)
        e1, bi, e2, r2 = _peer_select(scores, p_heads, _tile(n, 512, LANES))
        xs = _peer_dense(h, peer_u[l].astype(BF16), peer_v[l].astype(BF16), e1, bi, e2, r2, xs, gid, mod[5],
                         tm, tl=_tile(n, 2 * tm, tm), et=4 * nk, chunk=2 * nk)

    return xs.reshape(batch, seq, d)[:, t_ctx:]
```

```python
import functools
import math

import jax
import jax.numpy as jnp
from jax import lax
from jax.experimental import pallas as pl
from jax.experimental.pallas import tpu as pltpu

F32 = jnp.float32
BF16 = jnp.bfloat16

HEAD_DIM = 64
HEAD_SHIFT = HEAD_DIM.bit_length() - 1
ROPE_SHIFT = (HEAD_DIM // 4).bit_length() - 1
LANES = 128
GRID_W = 64
ROPE_THETA = 10000.0
NORM_EPS = 1e-6
GN_EPS = 64e-5
L2_EPS = 1e-12
PEER_TOPK = 16
LOG2_E = 1.4426950408889634
VMEM_LIMIT = 56 * 1024 * 1024


def _cparams(sem):
    return pltpu.CompilerParams(dimension_semantics=sem, vmem_limit_bytes=VMEM_LIMIT)


def _mm(a, b):
    return jnp.dot(a.astype(BF16), b.astype(BF16), preferred_element_type=F32)


def _segment_ones():
    r = lax.broadcasted_iota(jnp.int32, (LANES, LANES), 0) >> HEAD_SHIFT
    c = lax.broadcasted_iota(jnp.int32, (LANES, LANES), 1) >> HEAD_SHIFT
    return jnp.where(r == c, 1.0, 0.0).astype(BF16)


def _segsum(x, ones_bd):
    hi = x.astype(BF16)
    lo = (x - hi.astype(F32)).astype(BF16)
    return (jnp.dot(hi, ones_bd, preferred_element_type=F32)
            + jnp.dot(lo, ones_bd, preferred_element_type=F32))


def _tile(n, pref, align=8):
    for t in range(min(n, pref) // align * align, 0, -align):
        if n % t == 0:
            return t
    return n


def _mod_kernel(c_ref, w_ref, b_ref, o_ref):
    cv = c_ref[...]
    act = cv * jax.nn.sigmoid(cv)
    o_ref[...] = _mm(act, w_ref[...]) + b_ref[...]


def _modulation(c8, ada_w, ada_b):
    depth, d, n6 = ada_w.shape
    tn = _tile(n6, 1536)
    return pl.pallas_call(
        _mod_kernel,
        grid=(depth, n6 // tn),
        in_specs=[
            pl.BlockSpec((8, d), lambda l, j: (0, 0)),
            pl.BlockSpec((None, d, tn), lambda l, j: (l, 0, j)),
            pl.BlockSpec((None, 1, tn), lambda l, j: (l, 0, j)),
        ],
        out_specs=pl.BlockSpec((None, 8, tn), lambda l, j: (l, 0, j)),
        out_shape=jax.ShapeDtypeStruct((depth, 8, n6), F32),
        compiler_params=_cparams(("arbitrary", "arbitrary")),
        name="modulation",
    )(c8, ada_w, ada_b.reshape(depth, 1, n6))


def _norm_mod_kernel(gid_ref, x_ref, gain_ref, shift_ref, scale_ref, o_ref):
    del gid_ref
    x = x_ref[...]
    ms = jnp.mean(x * x, axis=-1, keepdims=True)
    y = x * lax.rsqrt(ms + NORM_EPS) * gain_ref[...]
    o_ref[...] = (y * (1.0 + scale_ref[...]) + shift_ref[...]).astype(BF16)


def _matmul_kernel(h_ref, w_ref, o_ref):
    o_ref[...] = jnp.dot(h_ref[...], w_ref[...], preferred_element_type=F32)


def _norm_proj_wide(x, gid, gain, shift, scale, w, tm, tm_mm, tn):
    n, d = x.shape
    nout = w.shape[1]
    h = pl.pallas_call(
        _norm_mod_kernel,
        grid_spec=pltpu.PrefetchScalarGridSpec(
            num_scalar_prefetch=1,
            grid=(n // tm,),
            in_specs=[
                pl.BlockSpec((tm, d), lambda i, g: (i, 0)),
                pl.BlockSpec((1, d), lambda i, g: (0, 0)),
                pl.BlockSpec((None, 1, d), lambda i, g: (g[i], 0, 0)),
                pl.BlockSpec((None, 1, d), lambda i, g: (g[i], 0, 0)),
            ],
            out_specs=pl.BlockSpec((tm, d), lambda i, g: (i, 0)),
        ),
        out_shape=jax.ShapeDtypeStruct((n, d), BF16),
        compiler_params=_cparams(("arbitrary",)),
        name="norm_mod",
    )(gid, x, gain.reshape(1, d), shift, scale)
    return pl.pallas_call(
        _matmul_kernel,
        grid=(nout // tn, n // tm_mm),
        in_specs=[
            pl.BlockSpec((tm_mm, d), lambda j, i: (i, 0)),
            pl.BlockSpec((d, tn), lambda j, i: (0, j)),
        ],
        out_specs=pl.BlockSpec((tm_mm, tn), lambda j, i: (i, j)),
        out_shape=jax.ShapeDtypeStruct((n, nout), F32),
        compiler_params=_cparams(("arbitrary", "arbitrary")),
        name="proj_matmul",
    )(h, w)


def _gated_proj_kernel(gid_ref, y1_ref, y2_ref, w1_ref, w2_ref, x_ref, gate_ref, o_ref):
    del gid_ref
    mix = (jnp.dot(y1_ref[...], w1_ref[...], preferred_element_type=F32)
           + jnp.dot(y2_ref[...], w2_ref[...], preferred_element_type=F32))
    o_ref[...] = x_ref[...] + gate_ref[...] * mix


def _gated_proj(y1, y2, w, x, gid, gate, tm):
    n, k1 = y1.shape
    k2 = y2.shape[1]
    d = w.shape[1]
    return pl.pallas_call(
        _gated_proj_kernel,
        grid_spec=pltpu.PrefetchScalarGridSpec(
            num_scalar_prefetch=1,
            grid=(n // tm,),
            in_specs=[
                pl.BlockSpec((tm, k1), lambda i, g: (i, 0)),
                pl.BlockSpec((tm, k2), lambda i, g: (i, 0)),
                pl.BlockSpec((k1, d), lambda i, g: (0, 0)),
                pl.BlockSpec((k2, d), lambda i, g: (k1 // k2, 0)),
                pl.BlockSpec((tm, d), lambda i, g: (i, 0)),
                pl.BlockSpec((None, 1, d), lambda i, g: (g[i], 0, 0)),
            ],
            out_specs=pl.BlockSpec((tm, d), lambda i, g: (i, 0)),
        ),
        out_shape=jax.ShapeDtypeStruct((n, d), F32),
        compiler_params=_cparams(("arbitrary",)),
        name="gated_proj",
    )(gid, y1, y2, w, w, x, gate)


def _rwkv_prep_kernel(first_ref, last_ref, x_ref, prev_ref, next_ref, tail_ref, cw_ref,
                      w0_ref, wup_ref, a0_ref, aup_ref, gup_ref, kk_ref, ka_ref, rk_ref,
                      q_out, v_out, bonus_out, gate_out, *, rw, lw, la):
    i = pl.program_id(0)
    tm = x_ref.shape[0]
    x = x_ref[...]
    row = lax.broadcasted_iota(jnp.int32, x.shape, 0)
    keep_prev = jnp.where(first_ref[i] == 1, 0.0, 1.0)
    keep_next = jnp.where(last_ref[i] == 1, 0.0, 1.0)
    halo_prev = prev_ref[7:8, :] * keep_prev
    halo_next = next_ref[0:1, :] * keep_next
    x_prev = jnp.where(row == 0, halo_prev, pltpu.roll(x, 1, axis=0))
    x_next = jnp.where(row == tm - 1, halo_next, pltpu.roll(x, tm - 1, axis=0))
    conv = x_prev * cw_ref[0:1, :] + x * cw_ref[1:2, :] + x_next * cw_ref[2:3, :]
    r = conv[:, 0:rw]
    k = conv[:, rw:2 * rw]
    v = conv[:, 2 * rw:3 * rw]
    tail = tail_ref[...]
    w_lo = jnp.tanh(tail[:, 0:lw])
    a_lo = tail[:, lw:lw + la]
    g_lo = jax.nn.sigmoid(tail[:, lw + la:])
    ones_bd = _segment_ones()

    def seg(t):
        return jnp.concatenate(
            [_segsum(t[:, c * LANES:(c + 1) * LANES], ones_bd) for c in range(rw // LANES)], axis=1)

    kap = k * kk_ref[...]
    nrm = jnp.sqrt(seg(kap * kap))
    kap = kap / jnp.maximum(nrm, L2_EPS)
    q_out[0] = r.T
    q_out[1] = kap.T
    v_out[...] = v.T
    gate_out[...] = _mm(g_lo, gup_ref[...])
    bonus = jnp.zeros_like(r)
    for d in range(2):
        z = w0_ref[d:d + 1, :] + _mm(w_lo, wup_ref[d])
        log_w = -jax.nn.softplus(-z) - 0.5
        q_out[2 + d] = jnp.exp(-jnp.exp(log_w)).T
        a = jax.nn.sigmoid(a0_ref[d:d + 1, :] + _mm(a_lo, aup_ref[d]))
        k_mod = k * (1.0 + (a - 1.0) * ka_ref[...])
        q_out[4 + d] = k_mod.T
        q_out[6 + d] = (kap * a).T
        bonus = bonus + seg(r * k_mod * rk_ref[...]) * v
    bonus_out[...] = bonus


def _rwkv_prep(p, first, last, conv_w, w0, w_up, a0, a_up, g_up, k_k, k_a, r_k, tm):
    n = p.shape[0]
    rw = k_k.shape[0]
    lw, la, lg = w_up.shape[1], a_up.shape[1], g_up.shape[0]
    tail_w = lw + la + lg
    tail_blk = (p.shape[1] - tail_w) // tail_w
    nb8 = n // 8
    row2 = lambda i, f, l: (0, 0)
    row3 = lambda i, f, l: (0, 0, 0)
    one = jax.ShapeDtypeStruct((n, rw), F32)
    one_spec = pl.BlockSpec((tm, rw), lambda i, f, l: (i, 0))
    return pl.pallas_call(
        functools.partial(_rwkv_prep_kernel, rw=rw, lw=lw, la=la),
        grid_spec=pltpu.PrefetchScalarGridSpec(
            num_scalar_prefetch=2,
            grid=(n // tm,),
            in_specs=[
                pl.BlockSpec((tm, 3 * rw), lambda i, f, l: (i, 0)),
                pl.BlockSpec((8, 3 * rw), lambda i, f, l: (jnp.maximum(i * (tm // 8) - 1, 0), 0)),
                pl.BlockSpec((8, 3 * rw), lambda i, f, l: (jnp.minimum((i + 1) * (tm // 8), nb8 - 1), 0)),
                pl.BlockSpec((tm, tail_w), lambda i, f, l: (i, tail_blk)),
                pl.BlockSpec((3, 3 * rw), row2),
                pl.BlockSpec((2, rw), row2),
                pl.BlockSpec((2, lw, rw), row3),
                pl.BlockSpec((2, rw), row2),
                pl.BlockSpec((2, la, rw), row3),
                pl.BlockSpec((lg, rw), row2),
                pl.BlockSpec((1, rw), row2),
                pl.BlockSpec((1, rw), row2),
                pl.BlockSpec((1, rw), row2),
            ],
            out_specs=[pl.BlockSpec((8, rw, tm), lambda i, f, l: (0, 0, i)),
                       pl.BlockSpec((rw, tm), lambda i, f, l: (0, i)), one_spec, one_spec],
        ),
        out_shape=[jax.ShapeDtypeStruct((8, rw, n), F32), jax.ShapeDtypeStruct((rw, n), F32), one, one],
        compiler_params=_cparams(("arbitrary",)),
        name="rwkv_prep",
    )(first, last, p, p, p, p, conv_w, w0, w_up, a0, a_up, g_up,
      k_k.reshape(1, rw), k_a.reshape(1, rw), r_k.reshape(1, rw))


def _to_streams_kernel(*refs, split_value):
    x_refs, o_ref = refs[:-1], refs[-1]
    heads = x_refs[0].shape[0] // HEAD_DIM
    rows = HEAD_DIM // 2 if split_value else HEAD_DIM
    for j in range(rows):
        second = j + HEAD_DIM // 2 if split_value else j
        parts = [x_ref[pl.ds(j, heads, stride=HEAD_DIM), :] for x_ref in x_refs]
        parts += [x_ref[pl.ds(second, heads, stride=HEAD_DIM), :] for x_ref in x_refs]
        slab = jnp.concatenate(parts, axis=0).T
        if split_value:
            o_ref[:, j, :] = slab
        else:
            o_ref[j] = slab


def _to_streams(q, v, batch, tm):
    nq, rw, n = q.shape
    seq = n // batch
    tiles = seq // tm
    qs = pl.pallas_call(
        functools.partial(_to_streams_kernel, split_value=False),
        grid=(nq, tiles),
        in_specs=[pl.BlockSpec((None, rw, tm), lambda a, i, b=b: (a, 0, b * tiles + i)) for b in range(batch)],
        out_specs=pl.BlockSpec((None, HEAD_DIM, tm, LANES), lambda a, i: (a, 0, i, 0)),
        out_shape=jax.ShapeDtypeStruct((nq, HEAD_DIM, seq, LANES), F32),
        compiler_params=_cparams(("arbitrary", "arbitrary")),
        name="to_streams",
    )(*([q] * batch))
    vs = pl.pallas_call(
        functools.partial(_to_streams_kernel, split_value=True),
        grid=(tiles,),
        in_specs=[pl.BlockSpec((rw, tm), lambda i, b=b: (0, b * tiles + i)) for b in range(batch)],
        out_specs=pl.BlockSpec((tm, HEAD_DIM // 2, LANES), lambda i: (i, 0, 0)),
        out_shape=jax.ShapeDtypeStruct((seq, HEAD_DIM // 2, LANES), F32),
        compiler_params=_cparams(("arbitrary",)),
        name="to_streams_v",
    )(*([v] * batch))
    return qs, vs


def _rwkv_scan_kernel(rf, kapf, wf, kf, bf, vf, rb, kapb, wb, kb, bb, vb, yf, yb, s_ref):
    hd, tc, ns = rf.shape
    half = vf.shape[1]

    @pl.when(pl.program_id(0) == 0)
    def _():
        s_ref[...] = jnp.zeros_like(s_ref)

    dirs = ((rf, kapf, wf, kf, bf, vf, yf), (rb, kapb, wb, kb, bb, vb, yb))

    def state_times_kappa(d, t):
        acc = jnp.zeros((half, ns), F32)
        for kk in range(hd):
            acc = acc + s_ref[d, kk] * dirs[d][1][kk, pl.ds(t, 1), :]
        return acc

    def update(d, t, t_next, sa):
        r_ref, kap_ref, w_ref, k_ref, b_ref, v_ref, y_ref = dirs[d]
        vv = v_ref[t]
        y = jnp.zeros((half, ns), F32)
        sa_next = jnp.zeros((half, ns), F32)
        for kk in range(hd):
            s_new = (s_ref[d, kk] * w_ref[kk, pl.ds(t, 1), :]
                     + (vv * k_ref[kk, pl.ds(t, 1), :] - sa * b_ref[kk, pl.ds(t, 1), :]))
            s_ref[d, kk] = s_new
            y = y + s_new * r_ref[kk, pl.ds(t, 1), :]
            sa_next = sa_next + s_new * kap_ref[kk, pl.ds(t_next, 1), :]
        y_ref[t] = y
        return sa_next

    def step(j, carry):
        sa_f, sa_b = carry
        sa_f = update(0, j, jnp.minimum(j + 1, tc - 1), sa_f)
        tb = tc - 1 - j
        sa_b = update(1, tb, jnp.maximum(tb - 1, 0), sa_b)
        return sa_f, sa_b

    lax.fori_loop(0, tc, step, (state_times_kappa(0, 0), state_times_kappa(1, tc - 1)))


def _rwkv_scan(qs, vs, t_ctx, tc):
    _, hd, seq, ns = qs.shape
    half = vs.shape[1]
    nb = seq // tc
    nbc = t_ctx // tc

    def back_block(i):
        return jnp.where(i < nbc, nbc - 1 - i, nb - 1 - (i - nbc))

    def qspec(a, back):
        if back:
            return pl.BlockSpec((None, hd, tc, ns), lambda i: (a, 0, back_block(i), 0))
        return pl.BlockSpec((None, hd, tc, ns), lambda i: (a, 0, i, 0))

    def vspec(back):
        if back:
            return pl.BlockSpec((tc, half, ns), lambda i: (back_block(i), 0, 0))
        return pl.BlockSpec((tc, half, ns), lambda i: (i, 0, 0))

    in_specs = ([qspec(a, False) for a in (0, 1, 2, 4, 6)] + [vspec(False)]
                + [qspec(a, True) for a in (0, 1, 3, 5, 7)] + [vspec(True)])
    out = jax.ShapeDtypeStruct((seq, half, ns), F32)
    return pl.pallas_call(
        _rwkv_scan_kernel,
        grid=(nb,),
        in_specs=in_specs,
        out_specs=[vspec(False), vspec(True)],
        out_shape=[out, out],
        scratch_shapes=[pltpu.VMEM((2, hd, half, ns), F32)],
        compiler_params=_cparams(("arbitrary",)),
        name="rwkv_scan",
    )(*([qs] * 5 + [vs] + [qs] * 5 + [vs]))


def _rwkv_finish_kernel(yf_ref, yb_ref, bonus_ref, gate_ref, gain_ref, bias_ref, o_ref, xt_ref):
    batch, tm, rw = bonus_ref.shape
    heads = rw // HEAD_DIM
    half = yf_ref.shape[1]
    for j in range(half):
        slab = (yf_ref[:, j, :] + yb_ref[:, j, :]).T
        for g in range(2 * batch):
            xt_ref[g % batch, pl.ds(j + half * (g // batch), heads, stride=HEAD_DIM), :] = (
                slab[g * heads:(g + 1) * heads])
    ones_bd = _segment_ones()
    for b in range(batch):
        y = xt_ref[b].T
        outs = []
        for c in range(rw // LANES):
            yc = y[:, c * LANES:(c + 1) * LANES]
            mu = _segsum(yc, ones_bd) * (1.0 / HEAD_DIM)
            dev = yc - mu
            var = _segsum(dev * dev, ones_bd) * (1.0 / HEAD_DIM)
            outs.append(dev * lax.rsqrt(var + GN_EPS))
        yn = jnp.concatenate(outs, axis=1)
        o_ref[b] = ((yn * gain_ref[...] + bias_ref[...] + bonus_ref[b]) * gate_ref[b]).astype(o_ref.dtype)


def _rwkv_finish(yf, yb, bonus, gate, gain, bias, batch, tm):
    seq, half, ns = yf.shape
    n, rw = bonus.shape
    yspec = pl.BlockSpec((tm, half, ns), lambda i: (i, 0, 0))
    spec = pl.BlockSpec((batch, tm, rw), lambda i: (0, i, 0))
    vec = pl.BlockSpec((1, rw), lambda i: (0, 0))
    out = pl.pallas_call(
        _rwkv_finish_kernel,
        grid=(seq // tm,),
        in_specs=[yspec, yspec, spec, spec, vec, vec],
        out_specs=spec,
        out_shape=jax.ShapeDtypeStruct((batch, seq, rw), BF16),
        scratch_shapes=[pltpu.VMEM((batch, rw, tm), F32)],
        compiler_params=_cparams(("arbitrary",)),
        name="rwkv_finish",
    )(yf, yb, bonus.reshape(batch, seq, rw), gate.reshape(batch, seq, rw),
      gain.reshape(1, rw), bias.reshape(1, rw))
    return out.reshape(n, rw)


def _da_prep_kernel(x_ref, cos_ref, sin_ref, qg_ref, kg_ref, o_ref):
    s = pl.program_id(1)

    @pl.when(s == 2)
    def _():
        o_ref[...] = x_ref[...].astype(BF16)

    @pl.when(s < 2)
    def _():
        ones_bd = _segment_ones()
        gain = jnp.where(s == 0, qg_ref[...] * (HEAD_DIM ** -0.5 * LOG2_E), kg_ref[...])
        cos = cos_ref[...]
        sin = sin_ref[...]
        lane = lax.broadcasted_iota(jnp.int32, cos.shape, 1)
        even = ((lane >> ROPE_SHIFT) & 1) == 0
        x = x_ref[...]
        for c in range(x.shape[1] // LANES):
            xc = x[:, c * LANES:(c + 1) * LANES]
            ms = _segsum(xc * xc, ones_bd) * (1.0 / HEAD_DIM)
            xn = xc * lax.rsqrt(ms + NORM_EPS) * gain
            swapped = jnp.where(even, pltpu.roll(xn, LANES - HEAD_DIM // 4, axis=1),
                                pltpu.roll(xn, HEAD_DIM // 4, axis=1))
            o_ref[:, c * LANES:(c + 1) * LANES] = (xn * cos + swapped * sin).astype(BF16)


def _da_prep(p, cos_t, sin_t, q_gain, k_gain, col0, width, tm):
    n = p.shape[0]
    blk0 = col0 // width
    qg = jnp.tile(q_gain, LANES // HEAD_DIM).reshape(1, LANES)
    kg = jnp.tile(k_gain, LANES // HEAD_DIM).reshape(1, LANES)
    return pl.pallas_call(
        _da_prep_kernel,
        grid=(n // tm, 3),
        in_specs=[
            pl.BlockSpec((tm, width), lambda i, s: (i, blk0 + s)),
            pl.BlockSpec((tm, LANES), lambda i, s: (i, 0)),
            pl.BlockSpec((tm, LANES), lambda i, s: (i, 0)),
            pl.BlockSpec((1, LANES), lambda i, s: (0, 0)),
            pl.BlockSpec((1, LANES), lambda i, s: (0, 0)),
        ],
        out_specs=pl.BlockSpec((None, tm, width), lambda i, s: (s, i, 0)),
        out_shape=jax.ShapeDtypeStruct((3, n, width), BF16),
        compiler_params=_cparams(("arbitrary", "arbitrary")),
        name="da_prep",
    )(p, cos_t, sin_t, qg, kg)


def _diff_attn_kernel(q_ref, k_ref, v_ref, lam_ref, sg_ref, o_ref, *, lam_init, t_ctx):
    tq = q_ref.shape[0]
    lv = lam_ref[...]
    lam = (jnp.exp(jnp.sum(lv[0:1] * lv[1:2], axis=1, keepdims=True))
           - jnp.exp(jnp.sum(lv[2:3] * lv[3:4], axis=1, keepdims=True)) + lam_init)

    def attend(keys, vals):
        q = q_ref[...]
        lane = lax.broadcasted_iota(jnp.int32, q.shape, 1)
        zero = jnp.zeros_like(q)
        outs = []
        for m in range(2):
            qm = jnp.where((lane >> HEAD_SHIFT) == m, q, zero)
            s = lax.dot_general(qm, keys, (((1,), (1,)), ((), ())), preferred_element_type=F32)
            p = jnp.exp2(s - jnp.max(s, axis=1, keepdims=True))
            den = jnp.sum(p, axis=1, keepdims=True)
            outs.append(jnp.dot(p.astype(BF16), vals, preferred_element_type=F32) / den)
        o = outs[0] - lam * outs[1]
        ms = jnp.mean(o * o, axis=1, keepdims=True)
        o_ref[...] = (o * lax.rsqrt(ms + NORM_EPS) * sg_ref[...] * (1.0 - lam_init)).astype(o_ref.dtype)

    is_ctx = pl.program_id(2) < t_ctx // tq

    @pl.when(is_ctx)
    def _():
        attend(k_ref[0:t_ctx, :], v_ref[0:t_ctx, :])

    @pl.when(jnp.logical_not(is_ctx))
    def _():
        attend(k_ref[...], v_ref[...])


def _diff_attention(qkv, lam_vec, sub_gain, lam_init, batch, t_ctx, tq):
    _, n, width = qkv.shape
    heads = width // LANES
    seq = n // batch
    nq = seq // tq
    return pl.pallas_call(
        functools.partial(_diff_attn_kernel, lam_init=lam_init, t_ctx=t_ctx),
        grid=(batch, heads, nq),
        in_specs=[
            pl.BlockSpec((None, tq, LANES), lambda b, h, i: (0, b * nq + i, h)),
            pl.BlockSpec((None, seq, LANES), lambda b, h, i: (1, b, h)),
            pl.BlockSpec((None, seq, LANES), lambda b, h, i: (2, b, h)),
            pl.BlockSpec((4, HEAD_DIM), lambda b, h, i: (0, 0)),
            pl.BlockSpec((1, LANES), lambda b, h, i: (0, 0)),
        ],
        out_specs=pl.BlockSpec((tq, LANES), lambda b, h, i: (b * nq + i, h)),
        out_shape=jax.ShapeDtypeStruct((n, width), BF16),
        compiler_params=_cparams(("arbitrary", "arbitrary", "arbitrary")),
        name="diff_attn",
    )(qkv, qkv, qkv, lam_vec, sub_gain.reshape(1, LANES))


def _peer_query_kernel(gid_ref, x_ref, gain_ref, shift_ref, scale_ref, w_ref, keys_ref, h_ref, s_ref):
    del gid_ref
    x = x_ref[...]
    ms = jnp.mean(x * x, axis=-1, keepdims=True)
    y = x * lax.rsqrt(ms + NORM_EPS) * gain_ref[...]
    h = (y * (1.0 + scale_ref[...]) + shift_ref[...]).astype(BF16)
    h_ref[...] = h
    q = jnp.dot(h, w_ref[...], preferred_element_type=F32)
    half = keys_ref.shape[2]
    for j in range(keys_ref.shape[0]):
        s_ref[j] = lax.dot_general(keys_ref[j], q[:, j * half:(j + 1) * half], (((1,), (1,)), ((), ())),
                                   precision=lax.Precision.HIGHEST, preferred_element_type=F32)


def _peer_query(x, gid, gain, shift, scale, w, keys, tm):
    n, d = x.shape
    hp, nk, half = keys.shape
    return pl.pallas_call(
        _peer_query_kernel,
        grid_spec=pltpu.PrefetchScalarGridSpec(
            num_scalar_prefetch=1,
            grid=(n // tm,),
            in_specs=[
                pl.BlockSpec((tm, d), lambda i, g: (i, 0)),
                pl.BlockSpec((1, d), lambda i, g: (0, 0)),
                pl.BlockSpec((None, 1, d), lambda i, g: (g[i], 0, 0)),
                pl.BlockSpec((None, 1, d), lambda i, g: (g[i], 0, 0)),
                pl.BlockSpec((d, hp * half), lambda i, g: (0, 0)),
                pl.BlockSpec((hp, nk, half), lambda i, g: (0, 0, 0)),
            ],
            out_specs=[pl.BlockSpec((tm, d), lambda i, g: (i, 0)),
                       pl.BlockSpec((hp, nk, tm), lambda i, g: (0, 0, i))],
        ),
        out_shape=[jax.ShapeDtypeStruct((n, d), BF16), jax.ShapeDtypeStruct((hp, nk, n), F32)],
        compiler_params=_cparams(("arbitrary",)),
        name="peer_query",
    )(gid, x, gain.reshape(1, d), shift, scale, w, keys)


SPLIT_B = 4


def _peer_select_kernel(s_ref, e1_ref, bi_ref, e2_ref, r2_ref, rank_ref, val_ref, sel_ref):
    nk, tl = s_ref.shape[1:]
    k = PEER_TOPK
    neg = -jnp.inf
    unranked = float(2 * k)

    def top(s, exact):
        key = lax.broadcasted_iota(jnp.int32, s.shape, 0)
        work = s
        rank = jnp.full(s.shape, unranked, F32)
        vals = []
        for r in range(k):
            m = jnp.max(work, axis=0, keepdims=True)
            hit = work == m
            if exact:
                hit = key == jnp.min(jnp.where(hit, key, nk), axis=0, keepdims=True)
            work = jnp.where(hit, neg, work)
            rank = jnp.where(hit, float(r), rank)
            vals.append(m)
        removed = jnp.sum(jnp.where(rank < unranked, 1.0, 0.0), axis=0, keepdims=True)
        return jnp.concatenate(vals, axis=0), rank, removed

    def stage1(exact):
        removed = jnp.zeros((1, tl), F32)
        for p in range(2):
            vals, rank, rem = top(s_ref[p], exact)
            val_ref[p] = vals
            rank_ref[p] = rank
            removed = jnp.maximum(removed, rem)
        return jnp.max(removed)

    sub = lax.broadcasted_iota(jnp.int32, (k, tl), 0)
    n_a = k // (SPLIT_B + 1)

    def stage2(exact):
        v1 = val_ref[0]
        v2 = val_ref[1]
        work, flat, valid = [], [], []
        for b in range(SPLIT_B):
            ok = sub <= k // (b + 1) - 1
            work.append(jnp.where(ok, v1 + v2[b:b + 1], neg))
            flat.append(jnp.where(ok, sub * k + b, -1))
            valid.append(ok)
        for a in range(n_a):
            ok = (sub >= SPLIT_B) & (sub <= k // (a + 1) - 1)
            work.append(jnp.where(ok, v1[a:a + 1] + v2, neg))
            flat.append(jnp.where(ok, sub + a * k, -1))
            valid.append(ok)
        top_sum = v1[0:1] + v2[0:1]
        z = jnp.zeros((1, tl), F32)
        for _ in range(k):
            m = jnp.max(functools.reduce(jnp.maximum, work), axis=0, keepdims=True)
            z = z + jnp.exp(m - top_sum)
            if exact:
                cand = functools.reduce(jnp.minimum, [jnp.where(w == m, f, k * k) for w, f in zip(work, flat)])
                idx = jnp.min(cand, axis=0, keepdims=True)
            for n in range(len(work)):
                hit = (flat[n] == idx) if exact else (work[n] == m)
                work[n] = jnp.where(hit, neg, work[n])
        count = [jnp.where(ok & (w == neg), 1.0, 0.0) for ok, w in zip(valid, work)]
        per_a = functools.reduce(jnp.add, count[:SPLIT_B])
        for a in range(n_a):
            extra = jnp.sum(count[SPLIT_B + a], axis=0, keepdims=True)
            per_a = per_a + jnp.where(sub == a, extra, 0.0)
        sel_ref[0:k, :] = per_a
        sel_ref[k:k + 1, :] = z
        return jnp.max(jnp.sum(per_a, axis=0, keepdims=True))

    @pl.when(stage1(False) > float(k))
    def _():
        stage1(True)

    @pl.when(stage2(False) > float(k))
    def _():
        stage2(True)

    s1 = s_ref[0]
    rank1 = rank_ref[0]
    bi = jnp.zeros(s1.shape, F32)
    for a in range(k):
        bi = jnp.where(rank1 == float(a), sel_ref[a:a + 1, :], bi)
    bi_ref[...] = bi
    e1_ref[...] = 0.5 * jnp.exp(s1 - val_ref[0, 0:1, :]) / sel_ref[k:k + 1, :]
    e2_ref[...] = jnp.exp(s_ref[1] - val_ref[1, 0:1, :]).astype(e2_ref.dtype)
    r2_ref[...] = rank_ref[1].astype(r2_ref.dtype)


def _peer_select(scores, heads, tl):
    hp, nk, n = scores.shape
    k = PEER_TOPK
    out = jax.ShapeDtypeStruct((heads, nk, n), F32)
    spec = pl.BlockSpec((None, nk, tl), lambda t, h: (h, 0, t))
    return pl.pallas_call(
        _peer_select_kernel,
        grid=(n // tl, heads),
        in_specs=[pl.BlockSpec((None, 2, nk, tl), lambda t, h: (h, 0, 0, t))],
        out_specs=[spec] * 4,
        out_shape=[out] * 4,
        scratch_shapes=[pltpu.VMEM((2, nk, tl), F32), pltpu.VMEM((2, k, tl), F32),
                        pltpu.VMEM((k + 8, tl), F32)],
        compiler_params=_cparams(("arbitrary", "arbitrary")),
        name="peer_select",
    )(scores.reshape(heads, 2, nk, n))


def _peer_dense_kernel(h_ref, u_ref, v_ref, e1_ref, bi_ref, e2_ref, r2_ref, o_ref, *, nk, chunk):
    e = pl.program_id(1)
    heads = e1_ref.shape[0]
    et = u_ref.shape[0]

    @pl.when(e == 0)
    def _():
        o_ref[...] = jnp.zeros_like(o_ref)

    total = None
    for c in range(et // chunk):
        a = lax.dot_general(u_ref[c * chunk:(c + 1) * chunk, :], h_ref[...], (((1,), (1,)), ((), ())),
                            preferred_element_type=F32)
        act = a * (1.0 + lax.erf(a * (2.0 ** -0.5)))
        parts = []
        for ii in range(chunk // nk):
            i = (e * et + c * chunk) // nk + ii
            g = jnp.zeros((nk, a.shape[1]), F32)
            for hh in range(heads):
                e1 = e1_ref[hh, pl.ds(i, 1), :]
                bi = bi_ref[hh, pl.ds(i, 1), :]
                g = g + jnp.where(r2_ref[hh] < bi, e2_ref[hh] * e1, 0.0)
            parts.append(g * act[ii * nk:(ii + 1) * nk])
        m = jnp.concatenate(parts, axis=0) if len(parts) > 1 else parts[0]
        contrib = jnp.dot(m.T.astype(BF16), v_ref[c * chunk:(c + 1) * chunk, :], preferred_element_type=F32)
        total = contrib if total is None else total + contrib
    o_ref[...] += total


def _peer_dense(h, u, v, e1, bi, e2, r2, tl, et, chunk):
    n, d = h.shape
    ne = u.shape[0]
    heads, nk, _ = e1.shape
    fac = pl.BlockSpec((heads, nk, tl), lambda t, e: (0, 0, t))
    return pl.pallas_call(
        functools.partial(_peer_dense_kernel, nk=nk, chunk=chunk),
        grid=(n // tl, ne // et),
        in_specs=[
            pl.BlockSpec((tl, d), lambda t, e: (t, 0)),
            pl.BlockSpec((et, d), lambda t, e: (e, 0)),
            pl.BlockSpec((et, d), lambda t, e: (e, 0)),
            fac, fac, fac, fac,
        ],
        out_specs=pl.BlockSpec((tl, d), lambda t, e: (t, 0)),
        out_shape=jax.ShapeDtypeStruct((n, d), F32),
        compiler_params=_cparams(("arbitrary", "arbitrary")),
        name="peer_dense",
    )(h, u, v, e1, bi, e2, r2)


def _gated_add_kernel(gid_ref, x_ref, y_ref, gate_ref, o_ref):
    del gid_ref
    o_ref[...] = x_ref[...] + gate_ref[...] * y_ref[...]


def _gated_add(x, y, gid, gate, tm):
    n, d = x.shape
    spec = pl.BlockSpec((tm, d), lambda i, g: (i, 0))
    return pl.pallas_call(
        _gated_add_kernel,
        grid_spec=pltpu.PrefetchScalarGridSpec(
            num_scalar_prefetch=1,
            grid=(n // tm,),
            in_specs=[spec, spec, pl.BlockSpec((None, 1, d), lambda i, g: (g[i], 0, 0))],
            out_specs=spec,
        ),
        out_shape=jax.ShapeDtypeStruct((n, d), F32),
        compiler_params=_cparams(("arbitrary",)),
        name="gated_add",
    )(gid, x, y, gate)


def _rope_tables(t_lat, t_ctx, batch):
    rows = t_lat // GRID_W
    row = jnp.repeat(jnp.arange(rows, dtype=jnp.int32), GRID_W)
    col = jnp.tile(jnp.arange(GRID_W, dtype=jnp.int32), rows)
    nfreq = HEAD_DIM // 4
    inv_freq = ROPE_THETA ** (-jnp.arange(nfreq, dtype=F32) / nfreq)
    ang_r = row.astype(F32)[:, None] * inv_freq
    ang_c = col.astype(F32)[:, None] * inv_freq
    cos = jnp.concatenate([jnp.cos(ang_r)] * 2 + [jnp.cos(ang_c)] * 2, axis=1)
    sin = jnp.concatenate([-jnp.sin(ang_r), jnp.sin(ang_r), -jnp.sin(ang_c), jnp.sin(ang_c)], axis=1)
    reps = LANES // HEAD_DIM
    cos = jnp.concatenate([jnp.ones((t_ctx, LANES), F32), jnp.tile(cos, (1, reps))], axis=0)
    sin = jnp.concatenate([jnp.zeros((t_ctx, LANES), F32), jnp.tile(sin, (1, reps))], axis=0)
    return jnp.tile(cos, (batch, 1)), jnp.tile(sin, (batch, 1))


def kernel(x, c, ctx, c_ctx, ada_w, ada_b, norm_gain, w_in, conv_w, rw_w0, rw_w_up, rw_a0, rw_a_up,
           rw_g_up, rw_k_k, rw_k_a, rw_r_k, rw_gn_gain, rw_gn_bias, da_q_gain, da_k_gain, da_lambda,
           da_sub_gain, w_out, peer_wq, peer_keys, peer_u, peer_v):
    batch, t_lat, d = x.shape
    t_ctx = ctx.shape[1]
    seq = t_ctx + t_lat
    n = batch * seq
    depth = ada_w.shape[0]
    rw = rw_k_k.shape[1]
    lora_w = rw_w_up.shape[2] + rw_a_up.shape[2] + rw_g_up.shape[1]
    da_w = w_out.shape[1] - rw
    p_heads, _, nk, half = peer_keys.shape[1:]
    assert 2 * batch * (rw // HEAD_DIM) == LANES, "the recurrence kernel fills the lanes with (half, batch, head)"

    tm = _tile(math.gcd(t_lat, t_ctx), 256)
    tile_pos = (jnp.arange(n // tm, dtype=jnp.int32) * tm) % seq
    gid = jnp.where(tile_pos < t_ctx, batch, jnp.arange(n // tm, dtype=jnp.int32) * tm // seq)
    first = ((tile_pos == 0) | (tile_pos == t_ctx)).astype(jnp.int32)
    last = ((tile_pos + tm == t_ctx) | (tile_pos + tm == seq)).astype(jnp.int32)

    xs = jnp.concatenate([ctx, x], axis=1).reshape(n, d)
    c8 = jnp.concatenate([c, c_ctx[None], jnp.zeros((8 - batch - 1, d), F32)], axis=0)
    mods = _modulation(c8, ada_w, ada_b)[:, :batch + 1].reshape(depth, batch + 1, 6, 1, d)
    cos_t, sin_t = _rope_tables(t_lat, t_ctx, batch)

    cols = jnp.concatenate([jnp.arange(3 * rw), jnp.arange(3 * rw + lora_w, w_in.shape[2]),
                            jnp.arange(3 * rw, 3 * rw + lora_w)])
    tn_in = _tile(w_in.shape[2], 1280, LANES)
    ts = _tile(tm, 128)

    for l in range(depth):
        lam_init = 0.8 - 0.6 * math.exp(-0.3 * l)
        mod = [mods[l, :, j] for j in range(6)]

        p = _norm_proj_wide(xs, gid, norm_gain[l, 0], mod[0], mod[1], w_in[l][:, cols].astype(BF16), tm,
                            _tile(n, 1024), tn_in)
        q8, v, bonus, gate = _rwkv_prep(
            p, first, last, conv_w[l], rw_w0[l], rw_w_up[l], rw_a0[l], rw_a_up[l], rw_g_up[l],
            rw_k_k[l], rw_k_a[l], rw_r_k[l].reshape(-1), tm)
        qs, vs = _to_streams(q8, v, batch, ts)
        yf, yb = _rwkv_scan(qs, vs, t_ctx, tc=_tile(math.gcd(t_lat, t_ctx), 32))
        rw_out = _rwkv_finish(yf, yb, bonus, gate, rw_gn_gain[l], rw_gn_bias[l], batch, ts)

        qkv = _da_prep(p, cos_t, sin_t, da_q_gain[l], da_k_gain[l], 3 * rw, da_w, tm)
        da_out = _diff_attention(qkv, da_lambda[l], da_sub_gain[l], lam_init, batch, t_ctx, tm)
        xs = _gated_proj(rw_out, da_out, w_out[l].astype(BF16), xs, gid, mod[2], tm)

        h, scores = _peer_query(xs, gid, norm_gain[l, 1], mod[3], mod[4], peer_wq[l].astype(BF16),
                                peer_keys[l].reshape(2 * p_heads, nk, half), tm)
        e1, bi, e2, r2 = _peer_select(scores, p_heads, _tile(n, 512, LANES))
        out = _peer_dense(h, peer_u[l].astype(BF16), peer_v[l].astype(BF16), e1, bi, e2, r2,
                          tl=_tile(n, 512, LANES), et=4 * nk, chunk=2 * nk)
        xs = _gated_add(xs, out, gid, mod[5], tm)

    return xs.reshape(batch, seq, d)[:, t_ctx:]
```

```python
import functools
import math

import jax
import jax.numpy as jnp
from jax import lax
from jax.experimental import pallas as pl
from jax.experimental.pallas import tpu as pltpu

F32 = jnp.float32
BF16 = jnp.bfloat16

HEAD_DIM = 64
HEAD_SHIFT = HEAD_DIM.bit_length() - 1
ROPE_SHIFT = (HEAD_DIM // 4).bit_length() - 1
LANES = 128
GRID_W = 64
ROPE_THETA = 10000.0
NORM_EPS = 1e-6
GN_EPS = 64e-5
L2_EPS = 1e-12
PEER_TOPK = 16
LOG2_E = 1.4426950408889634
VMEM_LIMIT = 56 * 1024 * 1024


def _cparams(sem):
    return pltpu.CompilerParams(dimension_semantics=sem, vmem_limit_bytes=VMEM_LIMIT)


def _mm(a, b):
    return jnp.dot(a.astype(BF16), b.astype(BF16), preferred_element_type=F32)


def _segment_ones():
    r = lax.broadcasted_iota(jnp.int32, (LANES, LANES), 0) >> HEAD_SHIFT
    c = lax.broadcasted_iota(jnp.int32, (LANES, LANES), 1) >> HEAD_SHIFT
    return jnp.where(r == c, 1.0, 0.0).astype(BF16)


def _segsum(x, ones_bd):
    hi = x.astype(BF16)
    lo = (x - hi.astype(F32)).astype(BF16)
    return (jnp.dot(hi, ones_bd, preferred_element_type=F32)
            + jnp.dot(lo, ones_bd, preferred_element_type=F32))


def _tile(n, pref, align=8):
    for t in range(min(n, pref) // align * align, 0, -align):
        if n % t == 0:
            return t
    return n


def _mod_kernel(c_ref, w_ref, b_ref, o_ref):
    cv = c_ref[...]
    act = cv * jax.nn.sigmoid(cv)
    o_ref[...] = _mm(act, w_ref[...]) + b_ref[...]


def _modulation(c8, ada_w, ada_b):
    depth, d, n6 = ada_w.shape
    tn = _tile(n6, 1536)
    return pl.pallas_call(
        _mod_kernel,
        grid=(depth, n6 // tn),
        in_specs=[
            pl.BlockSpec((8, d), lambda l, j: (0, 0)),
            pl.BlockSpec((None, d, tn), lambda l, j: (l, 0, j)),
            pl.BlockSpec((None, 1, tn), lambda l, j: (l, 0, j)),
        ],
        out_specs=pl.BlockSpec((None, 8, tn), lambda l, j: (l, 0, j)),
        out_shape=jax.ShapeDtypeStruct((depth, 8, n6), F32),
        compiler_params=_cparams(("arbitrary", "arbitrary")),
        name="modulation",
    )(c8, ada_w, ada_b.reshape(depth, 1, n6))


def _norm_mod_kernel(gid_ref, x_ref, gain_ref, shift_ref, scale_ref, o_ref):
    del gid_ref
    x = x_ref[...]
    ms = jnp.mean(x * x, axis=-1, keepdims=True)
    y = x * lax.rsqrt(ms + NORM_EPS) * gain_ref[...]
    o_ref[...] = (y * (1.0 + scale_ref[...]) + shift_ref[...]).astype(BF16)


def _matmul_kernel(h_ref, w_ref, o_ref):
    o_ref[...] = jnp.dot(h_ref[...], w_ref[...], preferred_element_type=F32)


def _norm_proj_wide(x, gid, gain, shift, scale, w, tm, tm_mm, tn):
    n, d = x.shape
    nout = w.shape[1]
    h = pl.pallas_call(
        _norm_mod_kernel,
        grid_spec=pltpu.PrefetchScalarGridSpec(
            num_scalar_prefetch=1,
            grid=(n // tm,),
            in_specs=[
                pl.BlockSpec((tm, d), lambda i, g: (i, 0)),
                pl.BlockSpec((1, d), lambda i, g: (0, 0)),
                pl.BlockSpec((None, 1, d), lambda i, g: (g[i], 0, 0)),
                pl.BlockSpec((None, 1, d), lambda i, g: (g[i], 0, 0)),
            ],
            out_specs=pl.BlockSpec((tm, d), lambda i, g: (i, 0)),
        ),
        out_shape=jax.ShapeDtypeStruct((n, d), BF16),
        compiler_params=_cparams(("arbitrary",)),
        name="norm_mod",
    )(gid, x, gain.reshape(1, d), shift, scale)
    return pl.pallas_call(
        _matmul_kernel,
        grid=(nout // tn, n // tm_mm),
        in_specs=[
            pl.BlockSpec((tm_mm, d), lambda j, i: (i, 0)),
            pl.BlockSpec((d, tn), lambda j, i: (0, j)),
        ],
        out_specs=pl.BlockSpec((tm_mm, tn), lambda j, i: (i, j)),
        out_shape=jax.ShapeDtypeStruct((n, nout), F32),
        compiler_params=_cparams(("arbitrary", "arbitrary")),
        name="proj_matmul",
    )(h, w)


def _gated_proj_kernel(gid_ref, y1_ref, y2_ref, w1_ref, w2_ref, x_ref, gate_ref, o_ref):
    del gid_ref
    mix = (jnp.dot(y1_ref[...], w1_ref[...], preferred_element_type=F32)
           + jnp.dot(y2_ref[...], w2_ref[...], preferred_element_type=F32))
    o_ref[...] = x_ref[...] + gate_ref[...] * mix


def _gated_proj(y1, y2, w, x, gid, gate, tm):
    n, k1 = y1.shape
    k2 = y2.shape[1]
    d = w.shape[1]
    return pl.pallas_call(
        _gated_proj_kernel,
        grid_spec=pltpu.PrefetchScalarGridSpec(
            num_scalar_prefetch=1,
            grid=(n // tm,),
            in_specs=[
                pl.BlockSpec((tm, k1), lambda i, g: (i, 0)),
                pl.BlockSpec((tm, k2), lambda i, g: (i, 0)),
                pl.BlockSpec((k1, d), lambda i, g: (0, 0)),
                pl.BlockSpec((k2, d), lambda i, g: (k1 // k2, 0)),
                pl.BlockSpec((tm, d), lambda i, g: (i, 0)),
                pl.BlockSpec((None, 1, d), lambda i, g: (g[i], 0, 0)),
            ],
            out_specs=pl.BlockSpec((tm, d), lambda i, g: (i, 0)),
        ),
        out_shape=jax.ShapeDtypeStruct((n, d), F32),
        compiler_params=_cparams(("arbitrary",)),
        name="gated_proj",
    )(gid, y1, y2, w, w, x, gate)


def _rwkv_prep_kernel(first_ref, last_ref, x_ref, prev_ref, next_ref, tail_ref, cw_ref,
                      w0_ref, wup_ref, a0_ref, aup_ref, gup_ref, kk_ref, ka_ref, rk_ref,
                      q_out, v_out, bonus_out, gate_out, *, rw, lw, la):
    i = pl.program_id(0)
    tm = x_ref.shape[0]
    x = x_ref[...]
    row = lax.broadcasted_iota(jnp.int32, x.shape, 0)
    keep_prev = jnp.where(first_ref[i] == 1, 0.0, 1.0)
    keep_next = jnp.where(last_ref[i] == 1, 0.0, 1.0)
    halo_prev = prev_ref[7:8, :] * keep_prev
    halo_next = next_ref[0:1, :] * keep_next
    x_prev = jnp.where(row == 0, halo_prev, pltpu.roll(x, 1, axis=0))
    x_next = jnp.where(row == tm - 1, halo_next, pltpu.roll(x, tm - 1, axis=0))
    conv = x_prev * cw_ref[0:1, :] + x * cw_ref[1:2, :] + x_next * cw_ref[2:3, :]
    r = conv[:, 0:rw]
    k = conv[:, rw:2 * rw]
    v = conv[:, 2 * rw:3 * rw]
    tail = tail_ref[...]
    w_lo = jnp.tanh(tail[:, 0:lw])
    a_lo = tail[:, lw:lw + la]
    g_lo = jax.nn.sigmoid(tail[:, lw + la:])
    ones_bd = _segment_ones()

    def seg(t):
        return jnp.concatenate(
            [_segsum(t[:, c * LANES:(c + 1) * LANES], ones_bd) for c in range(rw // LANES)], axis=1)

    kap = k * kk_ref[...]
    nrm = jnp.sqrt(seg(kap * kap))
    kap = kap / jnp.maximum(nrm, L2_EPS)
    q_out[0] = r
    q_out[1] = kap
    v_out[...] = v
    gate_out[...] = _mm(g_lo, gup_ref[...])
    bonus = jnp.zeros_like(r)
    for d in range(2):
        z = w0_ref[d:d + 1, :] + _mm(w_lo, wup_ref[d])
        log_w = -jax.nn.softplus(-z) - 0.5
        q_out[2 + d] = jnp.exp(-jnp.exp(log_w))
        a = jax.nn.sigmoid(a0_ref[d:d + 1, :] + _mm(a_lo, aup_ref[d]))
        k_mod = k * (1.0 + (a - 1.0) * ka_ref[...])
        q_out[4 + d] = k_mod
        q_out[6 + d] = kap * a
        bonus = bonus + seg(r * k_mod * rk_ref[...]) * v
    bonus_out[...] = bonus


def _rwkv_prep(p, first, last, conv_w, w0, w_up, a0, a_up, g_up, k_k, k_a, r_k, tm):
    n = p.shape[0]
    rw = k_k.shape[0]
    lw, la, lg = w_up.shape[1], a_up.shape[1], g_up.shape[0]
    tail_w = lw + la + lg
    tail_blk = (p.shape[1] - tail_w) // tail_w
    nb8 = n // 8
    row2 = lambda i, f, l: (0, 0)
    row3 = lambda i, f, l: (0, 0, 0)
    one = jax.ShapeDtypeStruct((n, rw), F32)
    one_spec = pl.BlockSpec((tm, rw), lambda i, f, l: (i, 0))
    return pl.pallas_call(
        functools.partial(_rwkv_prep_kernel, rw=rw, lw=lw, la=la),
        grid_spec=pltpu.PrefetchScalarGridSpec(
            num_scalar_prefetch=2,
            grid=(n // tm,),
            in_specs=[
                pl.BlockSpec((tm, 3 * rw), lambda i, f, l: (i, 0)),
                pl.BlockSpec((8, 3 * rw), lambda i, f, l: (jnp.maximum(i * (tm // 8) - 1, 0), 0)),
                pl.BlockSpec((8, 3 * rw), lambda i, f, l: (jnp.minimum((i + 1) * (tm // 8), nb8 - 1), 0)),
                pl.BlockSpec((tm, tail_w), lambda i, f, l: (i, tail_blk)),
                pl.BlockSpec((3, 3 * rw), row2),
                pl.BlockSpec((2, rw), row2),
                pl.BlockSpec((2, lw, rw), row3),
                pl.BlockSpec((2, rw), row2),
                pl.BlockSpec((2, la, rw), row3),
                pl.BlockSpec((lg, rw), row2),
                pl.BlockSpec((1, rw), row2),
                pl.BlockSpec((1, rw), row2),
                pl.BlockSpec((1, rw), row2),
            ],
            out_specs=[pl.BlockSpec((8, tm, rw), lambda i, f, l: (0, i, 0)), one_spec, one_spec, one_spec],
        ),
        out_shape=[jax.ShapeDtypeStruct((8, n, rw), F32), one, one, one],
        compiler_params=_cparams(("arbitrary",)),
        name="rwkv_prep",
    )(first, last, p, p, p, p, conv_w, w0, w_up, a0, a_up, g_up,
      k_k.reshape(1, rw), k_a.reshape(1, rw), r_k.reshape(1, rw))


def _to_streams_kernel(x_ref, o_ref, xt_ref, *, split_value):
    batch, tm, rw = x_ref.shape
    heads = rw // HEAD_DIM
    for b in range(batch):
        xt_ref[b] = x_ref[b].T
    rows = HEAD_DIM // 2 if split_value else HEAD_DIM
    for j in range(rows):
        second = j + HEAD_DIM // 2 if split_value else j
        parts = [xt_ref[b, pl.ds(j, heads, stride=HEAD_DIM), :] for b in range(batch)]
        parts += [xt_ref[b, pl.ds(second, heads, stride=HEAD_DIM), :] for b in range(batch)]
        slab = jnp.concatenate(parts, axis=0).T
        if split_value:
            o_ref[:, j, :] = slab
        else:
            o_ref[j] = slab


def _to_streams(q, v, batch, tm):
    nq, n, rw = q.shape
    seq = n // batch
    scratch = [pltpu.VMEM((batch, rw, tm), F32)]
    qs = pl.pallas_call(
        functools.partial(_to_streams_kernel, split_value=False),
        grid=(nq, seq // tm),
        in_specs=[pl.BlockSpec((None, batch, tm, rw), lambda a, i: (a, 0, i, 0))],
        out_specs=pl.BlockSpec((None, HEAD_DIM, tm, LANES), lambda a, i: (a, 0, i, 0)),
        out_shape=jax.ShapeDtypeStruct((nq, HEAD_DIM, seq, LANES), F32),
        scratch_shapes=scratch,
        compiler_params=_cparams(("arbitrary", "arbitrary")),
        name="to_streams",
    )(q.reshape(nq, batch, seq, rw))
    vs = pl.pallas_call(
        functools.partial(_to_streams_kernel, split_value=True),
        grid=(seq // tm,),
        in_specs=[pl.BlockSpec((batch, tm, rw), lambda i: (0, i, 0))],
        out_specs=pl.BlockSpec((tm, HEAD_DIM // 2, LANES), lambda i: (i, 0, 0)),
        out_shape=jax.ShapeDtypeStruct((seq, HEAD_DIM // 2, LANES), F32),
        scratch_shapes=scratch,
        compiler_params=_cparams(("arbitrary",)),
        name="to_streams_v",
    )(v.reshape(batch, seq, rw))
    return qs, vs


def _rwkv_scan_kernel(rf, kapf, wf, kf, bf, vf, rb, kapb, wb, kb, bb, vb, yf, yb, s_ref):
    hd, tc, ns = rf.shape
    half = vf.shape[1]

    @pl.when(pl.program_id(0) == 0)
    def _():
        s_ref[...] = jnp.zeros_like(s_ref)

    dirs = ((rf, kapf, wf, kf, bf, vf, yf), (rb, kapb, wb, kb, bb, vb, yb))

    def state_times_kappa(d, t):
        acc = jnp.zeros((half, ns), F32)
        for kk in range(hd):
            acc = acc + s_ref[d, kk] * dirs[d][1][kk, pl.ds(t, 1), :]
        return acc

    def update(d, t, t_next, sa):
        r_ref, kap_ref, w_ref, k_ref, b_ref, v_ref, y_ref = dirs[d]
        vv = v_ref[t]
        y = jnp.zeros((half, ns), F32)
        sa_next = jnp.zeros((half, ns), F32)
        for kk in range(hd):
            s_new = (s_ref[d, kk] * w_ref[kk, pl.ds(t, 1), :]
                     + (vv * k_ref[kk, pl.ds(t, 1), :] - sa * b_ref[kk, pl.ds(t, 1), :]))
            s_ref[d, kk] = s_new
            y = y + s_new * r_ref[kk, pl.ds(t, 1), :]
            sa_next = sa_next + s_new * kap_ref[kk, pl.ds(t_next, 1), :]
        y_ref[t] = y
        return sa_next

    def step(j, carry):
        sa_f, sa_b = carry
        sa_f = update(0, j, jnp.minimum(j + 1, tc - 1), sa_f)
        tb = tc - 1 - j
        sa_b = update(1, tb, jnp.maximum(tb - 1, 0), sa_b)
        return sa_f, sa_b

    lax.fori_loop(0, tc, step, (state_times_kappa(0, 0), state_times_kappa(1, tc - 1)))


def _rwkv_scan(qs, vs, t_ctx, tc):
    _, hd, seq, ns = qs.shape
    half = vs.shape[1]
    nb = seq // tc
    nbc = t_ctx // tc

    def back_block(i):
        return jnp.where(i < nbc, nbc - 1 - i, nb - 1 - (i - nbc))

    def qspec(a, back):
        if back:
            return pl.BlockSpec((None, hd, tc, ns), lambda i: (a, 0, back_block(i), 0))
        return pl.BlockSpec((None, hd, tc, ns), lambda i: (a, 0, i, 0))

    def vspec(back):
        if back:
            return pl.BlockSpec((tc, half, ns), lambda i: (back_block(i), 0, 0))
        return pl.BlockSpec((tc, half, ns), lambda i: (i, 0, 0))

    in_specs = ([qspec(a, False) for a in (0, 1, 2, 4, 6)] + [vspec(False)]
                + [qspec(a, True) for a in (0, 1, 3, 5, 7)] + [vspec(True)])
    out = jax.ShapeDtypeStruct((seq, half, ns), F32)
    return pl.pallas_call(
        _rwkv_scan_kernel,
        grid=(nb,),
        in_specs=in_specs,
        out_specs=[vspec(False), vspec(True)],
        out_shape=[out, out],
        scratch_shapes=[pltpu.VMEM((2, hd, half, ns), F32)],
        compiler_params=_cparams(("arbitrary",)),
        name="rwkv_scan",
    )(*([qs] * 5 + [vs] + [qs] * 5 + [vs]))


def _rwkv_finish_kernel(yf_ref, yb_ref, bonus_ref, gate_ref, gain_ref, bias_ref, o_ref, xt_ref):
    batch, tm, rw = bonus_ref.shape
    heads = rw // HEAD_DIM
    half = yf_ref.shape[1]
    for j in range(half):
        slab = (yf_ref[:, j, :] + yb_ref[:, j, :]).T
        for g in range(2 * batch):
            xt_ref[g % batch, pl.ds(j + half * (g // batch), heads, stride=HEAD_DIM), :] = (
                slab[g * heads:(g + 1) * heads])
    ones_bd = _segment_ones()
    for b in range(batch):
        y = xt_ref[b].T
        outs = []
        for c in range(rw // LANES):
            yc = y[:, c * LANES:(c + 1) * LANES]
            mu = _segsum(yc, ones_bd) * (1.0 / HEAD_DIM)
            dev = yc - mu
            var = _segsum(dev * dev, ones_bd) * (1.0 / HEAD_DIM)
            outs.append(dev * lax.rsqrt(var + GN_EPS))
        yn = jnp.concatenate(outs, axis=1)
        o_ref[b] = ((yn * gain_ref[...] + bias_ref[...] + bonus_ref[b]) * gate_ref[b]).astype(o_ref.dtype)


def _rwkv_finish(yf, yb, bonus, gate, gain, bias, batch, tm):
    seq, half, ns = yf.shape
    n, rw = bonus.shape
    yspec = pl.BlockSpec((tm, half, ns), lambda i: (i, 0, 0))
    spec = pl.BlockSpec((batch, tm, rw), lambda i: (0, i, 0))
    vec = pl.BlockSpec((1, rw), lambda i: (0, 0))
    out = pl.pallas_call(
        _rwkv_finish_kernel,
        grid=(seq // tm,),
        in_specs=[yspec, yspec, spec, spec, vec, vec],
        out_specs=spec,
        out_shape=jax.ShapeDtypeStruct((batch, seq, rw), BF16),
        scratch_shapes=[pltpu.VMEM((batch, rw, tm), F32)],
        compiler_params=_cparams(("arbitrary",)),
        name="rwkv_finish",
    )(yf, yb, bonus.reshape(batch, seq, rw), gate.reshape(batch, seq, rw),
      gain.reshape(1, rw), bias.reshape(1, rw))
    return out.reshape(n, rw)


def _da_prep_kernel(x_ref, cos_ref, sin_ref, qg_ref, kg_ref, o_ref):
    s = pl.program_id(1)

    @pl.when(s == 2)
    def _():
        o_ref[...] = x_ref[...].astype(BF16)

    @pl.when(s < 2)
    def _():
        ones_bd = _segment_ones()
        gain = jnp.where(s == 0, qg_ref[...] * (HEAD_DIM ** -0.5 * LOG2_E), kg_ref[...])
        cos = cos_ref[...]
        sin = sin_ref[...]
        lane = lax.broadcasted_iota(jnp.int32, cos.shape, 1)
        even = ((lane >> ROPE_SHIFT) & 1) == 0
        x = x_ref[...]
        for c in range(x.shape[1] // LANES):
            xc = x[:, c * LANES:(c + 1) * LANES]
            ms = _segsum(xc * xc, ones_bd) * (1.0 / HEAD_DIM)
            xn = xc * lax.rsqrt(ms + NORM_EPS) * gain
            swapped = jnp.where(even, pltpu.roll(xn, LANES - HEAD_DIM // 4, axis=1),
                                pltpu.roll(xn, HEAD_DIM // 4, axis=1))
            o_ref[:, c * LANES:(c + 1) * LANES] = (xn * cos + swapped * sin).astype(BF16)


def _da_prep(p, cos_t, sin_t, q_gain, k_gain, col0, width, tm):
    n = p.shape[0]
    blk0 = col0 // width
    qg = jnp.tile(q_gain, LANES // HEAD_DIM).reshape(1, LANES)
    kg = jnp.tile(k_gain, LANES // HEAD_DIM).reshape(1, LANES)
    return pl.pallas_call(
        _da_prep_kernel,
        grid=(n // tm, 3),
        in_specs=[
            pl.BlockSpec((tm, width), lambda i, s: (i, blk0 + s)),
            pl.BlockSpec((tm, LANES), lambda i, s: (i, 0)),
            pl.BlockSpec((tm, LANES), lambda i, s: (i, 0)),
            pl.BlockSpec((1, LANES), lambda i, s: (0, 0)),
            pl.BlockSpec((1, LANES), lambda i, s: (0, 0)),
        ],
        out_specs=pl.BlockSpec((None, tm, width), lambda i, s: (s, i, 0)),
        out_shape=jax.ShapeDtypeStruct((3, n, width), BF16),
        compiler_params=_cparams(("arbitrary", "arbitrary")),
        name="da_prep",
    )(p, cos_t, sin_t, qg, kg)


def _diff_attn_kernel(q_ref, k_ref, v_ref, lam_ref, sg_ref, o_ref, *, lam_init, t_ctx):
    tq = q_ref.shape[0]
    lv = lam_ref[...]
    lam = (jnp.exp(jnp.sum(lv[0:1] * lv[1:2], axis=1, keepdims=True))
           - jnp.exp(jnp.sum(lv[2:3] * lv[3:4], axis=1, keepdims=True)) + lam_init)

    def attend(keys, vals):
        q = q_ref[...]
        lane = lax.broadcasted_iota(jnp.int32, q.shape, 1)
        zero = jnp.zeros_like(q)
        outs = []
        for m in range(2):
            qm = jnp.where((lane >> HEAD_SHIFT) == m, q, zero)
            s = lax.dot_general(qm, keys, (((1,), (1,)), ((), ())), preferred_element_type=F32)
            p = jnp.exp2(s - jnp.max(s, axis=1, keepdims=True))
            den = jnp.sum(p, axis=1, keepdims=True)
            outs.append(jnp.dot(p.astype(BF16), vals, preferred_element_type=F32) / den)
        o = outs[0] - lam * outs[1]
        ms = jnp.mean(o * o, axis=1, keepdims=True)
        o_ref[...] = (o * lax.rsqrt(ms + NORM_EPS) * sg_ref[...] * (1.0 - lam_init)).astype(o_ref.dtype)

    is_ctx = pl.program_id(2) < t_ctx // tq

    @pl.when(is_ctx)
    def _():
        attend(k_ref[0:t_ctx, :], v_ref[0:t_ctx, :])

    @pl.when(jnp.logical_not(is_ctx))
    def _():
        attend(k_ref[...], v_ref[...])


def _diff_attention(qkv, lam_vec, sub_gain, lam_init, batch, t_ctx, tq):
    _, n, width = qkv.shape
    heads = width // LANES
    seq = n // batch
    nq = seq // tq
    return pl.pallas_call(
        functools.partial(_diff_attn_kernel, lam_init=lam_init, t_ctx=t_ctx),
        grid=(batch, heads, nq),
        in_specs=[
            pl.BlockSpec((None, tq, LANES), lambda b, h, i: (0, b * nq + i, h)),
            pl.BlockSpec((None, seq, LANES), lambda b, h, i: (1, b, h)),
            pl.BlockSpec((None, seq, LANES), lambda b, h, i: (2, b, h)),
            pl.BlockSpec((4, HEAD_DIM), lambda b, h, i: (0, 0)),
            pl.BlockSpec((1, LANES), lambda b, h, i: (0, 0)),
        ],
        out_specs=pl.BlockSpec((tq, LANES), lambda b, h, i: (b * nq + i, h)),
        out_shape=jax.ShapeDtypeStruct((n, width), BF16),
        compiler_params=_cparams(("arbitrary", "arbitrary", "arbitrary")),
        name="diff_attn",
    )(qkv, qkv, qkv, lam_vec, sub_gain.reshape(1, LANES))


def _peer_query_kernel(gid_ref, x_ref, gain_ref, shift_ref, scale_ref, w_ref, keys_ref, h_ref, s_ref):
    del gid_ref
    x = x_ref[...]
    ms = jnp.mean(x * x, axis=-1, keepdims=True)
    y = x * lax.rsqrt(ms + NORM_EPS) * gain_ref[...]
    h = (y * (1.0 + scale_ref[...]) + shift_ref[...]).astype(BF16)
    h_ref[...] = h
    q = jnp.dot(h, w_ref[...], preferred_element_type=F32)
    half = keys_ref.shape[2]
    for j in range(keys_ref.shape[0]):
        s_ref[j] = lax.dot_general(keys_ref[j], q[:, j * half:(j + 1) * half], (((1,), (1,)), ((), ())),
                                   precision=lax.Precision.HIGHEST, preferred_element_type=F32)


def _peer_query(x, gid, gain, shift, scale, w, keys, tm):
    n, d = x.shape
    hp, nk, half = keys.shape
    return pl.pallas_call(
        _peer_query_kernel,
        grid_spec=pltpu.PrefetchScalarGridSpec(
            num_scalar_prefetch=1,
            grid=(n // tm,),
            in_specs=[
                pl.BlockSpec((tm, d), lambda i, g: (i, 0)),
                pl.BlockSpec((1, d), lambda i, g: (0, 0)),
                pl.BlockSpec((None, 1, d), lambda i, g: (g[i], 0, 0)),
                pl.BlockSpec((None, 1, d), lambda i, g: (g[i], 0, 0)),
                pl.BlockSpec((d, hp * half), lambda i, g: (0, 0)),
                pl.BlockSpec((hp, nk, half), lambda i, g: (0, 0, 0)),
            ],
            out_specs=[pl.BlockSpec((tm, d), lambda i, g: (i, 0)),
                       pl.BlockSpec((hp, nk, tm), lambda i, g: (0, 0, i))],
        ),
        out_shape=[jax.ShapeDtypeStruct((n, d), BF16), jax.ShapeDtypeStruct((hp, nk, n), F32)],
        compiler_params=_cparams(("arbitrary",)),
        name="peer_query",
    )(gid, x, gain.reshape(1, d), shift, scale, w, keys)


SPLIT_B = 4


def _peer_select_kernel(s_ref, e1_ref, bi_ref, e2_ref, r2_ref, rank_ref, val_ref, sel_ref):
    nk, tl = s_ref.shape[1:]
    k = PEER_TOPK
    neg = -jnp.inf
    unranked = float(2 * k)

    def top(s, exact):
        key = lax.broadcasted_iota(jnp.int32, s.shape, 0)
        work = s
        rank = jnp.full(s.shape, unranked, F32)
        vals = []
        for r in range(k):
            m = jnp.max(work, axis=0, keepdims=True)
            hit = work == m
            if exact:
                hit = key == jnp.min(jnp.where(hit, key, nk), axis=0, keepdims=True)
            work = jnp.where(hit, neg, work)
            rank = jnp.where(hit, float(r), rank)
            vals.append(m)
        removed = jnp.sum(jnp.where(rank < unranked, 1.0, 0.0), axis=0, keepdims=True)
        return jnp.concatenate(vals, axis=0), rank, removed

    def stage1(exact):
        removed = jnp.zeros((1, tl), F32)
        for p in range(2):
            vals, rank, rem = top(s_ref[p], exact)
            val_ref[p] = vals
            rank_ref[p] = rank
            removed = jnp.maximum(removed, rem)
        return jnp.max(removed)

    sub = lax.broadcasted_iota(jnp.int32, (k, tl), 0)
    n_a = k // (SPLIT_B + 1)

    def stage2(exact):
        v1 = val_ref[0]
        v2 = val_ref[1]
        work, flat, valid = [], [], []
        for b in range(SPLIT_B):
            ok = sub <= k // (b + 1) - 1
            work.append(jnp.where(ok, v1 + v2[b:b + 1], neg))
            flat.append(jnp.where(ok, sub * k + b, -1))
            valid.append(ok)
        for a in range(n_a):
            ok = (sub >= SPLIT_B) & (sub <= k // (a + 1) - 1)
            work.append(jnp.where(ok, v1[a:a + 1] + v2, neg))
            flat.append(jnp.where(ok, sub + a * k, -1))
            valid.append(ok)
        top_sum = v1[0:1] + v2[0:1]
        z = jnp.zeros((1, tl), F32)
        for _ in range(k):
            m = jnp.max(functools.reduce(jnp.maximum, work), axis=0, keepdims=True)
            z = z + jnp.exp(m - top_sum)
            if exact:
                cand = functools.reduce(jnp.minimum, [jnp.where(w == m, f, k * k) for w, f in zip(work, flat)])
                idx = jnp.min(cand, axis=0, keepdims=True)
            for n in range(len(work)):
                hit = (flat[n] == idx) if exact else (work[n] == m)
                work[n] = jnp.where(hit, neg, work[n])
        count = [jnp.where(ok & (w == neg), 1.0, 0.0) for ok, w in zip(valid, work)]
        per_a = functools.reduce(jnp.add, count[:SPLIT_B])
        for a in range(n_a):
            extra = jnp.sum(count[SPLIT_B + a], axis=0, keepdims=True)
            per_a = per_a + jnp.where(sub == a, extra, 0.0)
        sel_ref[0:k, :] = per_a
        sel_ref[k:k + 1, :] = z
        return jnp.max(jnp.sum(per_a, axis=0, keepdims=True))

    @pl.when(stage1(False) > float(k))
    def _():
        stage1(True)

    @pl.when(stage2(False) > float(k))
    def _():
        stage2(True)

    s1 = s_ref[0]
    rank1 = rank_ref[0]
    bi = jnp.zeros(s1.shape, F32)
    for a in range(k):
        bi = jnp.where(rank1 == float(a), sel_ref[a:a + 1, :], bi)
    bi_ref[...] = bi
    e1_ref[...] = 0.5 * jnp.exp(s1 - val_ref[0, 0:1, :]) / sel_ref[k:k + 1, :]
    e2_ref[...] = jnp.exp(s_ref[1] - val_ref[1, 0:1, :]).astype(e2_ref.dtype)
    r2_ref[...] = rank_ref[1].astype(r2_ref.dtype)


def _peer_select(scores, heads, tl):
    hp, nk, n = scores.shape
    k = PEER_TOPK
    out = jax.ShapeDtypeStruct((heads, nk, n), F32)
    spec = pl.BlockSpec((None, nk, tl), lambda t, h: (h, 0, t))
    return pl.pallas_call(
        _peer_select_kernel,
        grid=(n // tl, heads),
        in_specs=[pl.BlockSpec((None, 2, nk, tl), lambda t, h: (h, 0, 0, t))],
        out_specs=[spec] * 4,
        out_shape=[out] * 4,
        scratch_shapes=[pltpu.VMEM((2, nk, tl), F32), pltpu.VMEM((2, k, tl), F32),
                        pltpu.VMEM((k + 8, tl), F32)],
        compiler_params=_cparams(("arbitrary", "arbitrary")),
        name="peer_select",
    )(scores.reshape(heads, 2, nk, n))


def _peer_dense_kernel(h_ref, u_ref, v_ref, e1_ref, bi_ref, e2_ref, r2_ref, o_ref, *, nk, chunk):
    e = pl.program_id(1)
    heads = e1_ref.shape[0]
    et = u_ref.shape[0]

    @pl.when(e == 0)
    def _():
        o_ref[...] = jnp.zeros_like(o_ref)

    total = None
    for c in range(et // chunk):
        a = lax.dot_general(u_ref[c * chunk:(c + 1) * chunk, :], h_ref[...], (((1,), (1,)), ((), ())),
                            preferred_element_type=F32)
        act = a * (1.0 + lax.erf(a * (2.0 ** -0.5)))
        parts = []
        for ii in range(chunk // nk):
            i = (e * et + c * chunk) // nk + ii
            g = jnp.zeros((nk, a.shape[1]), F32)
            for hh in range(heads):
                e1 = e1_ref[hh, pl.ds(i, 1), :]
                bi = bi_ref[hh, pl.ds(i, 1), :]
                g = g + jnp.where(r2_ref[hh] < bi, e2_ref[hh] * e1, 0.0)
            parts.append(g * act[ii * nk:(ii + 1) * nk])
        m = jnp.concatenate(parts, axis=0) if len(parts) > 1 else parts[0]
        contrib = jnp.dot(m.T.astype(BF16), v_ref[c * chunk:(c + 1) * chunk, :], preferred_element_type=F32)
        total = contrib if total is None else total + contrib
    o_ref[...] += total


def _peer_dense(h, u, v, e1, bi, e2, r2, tl, et, chunk):
    n, d = h.shape
    ne = u.shape[0]
    heads, nk, _ = e1.shape
    fac = pl.BlockSpec((heads, nk, tl), lambda t, e: (0, 0, t))
    return pl.pallas_call(
        functools.partial(_peer_dense_kernel, nk=nk, chunk=chunk),
        grid=(n // tl, ne // et),
        in_specs=[
            pl.BlockSpec((tl, d), lambda t, e: (t, 0)),
            pl.BlockSpec((et, d), lambda t, e: (e, 0)),
            pl.BlockSpec((et, d), lambda t, e: (e, 0)),
            fac, fac, fac, fac,
        ],
        out_specs=pl.BlockSpec((tl, d), lambda t, e: (t, 0)),
        out_shape=jax.ShapeDtypeStruct((n, d), F32),
        compiler_params=_cparams(("arbitrary", "arbitrary")),
        name="peer_dense",
    )(h, u, v, e1, bi, e2, r2)


def _gated_add_kernel(gid_ref, x_ref, y_ref, gate_ref, o_ref):
    del gid_ref
    o_ref[...] = x_ref[...] + gate_ref[...] * y_ref[...]


def _gated_add(x, y, gid, gate, tm):
    n, d = x.shape
    spec = pl.BlockSpec((tm, d), lambda i, g: (i, 0))
    return pl.pallas_call(
        _gated_add_kernel,
        grid_spec=pltpu.PrefetchScalarGridSpec(
            num_scalar_prefetch=1,
            grid=(n // tm,),
            in_specs=[spec, spec, pl.BlockSpec((None, 1, d), lambda i, g: (g[i], 0, 0))],
            out_specs=spec,
        ),
        out_shape=jax.ShapeDtypeStruct((n, d), F32),
        compiler_params=_cparams(("arbitrary",)),
        name="gated_add",
    )(gid, x, y, gate)


def _rope_tables(t_lat, t_ctx, batch):
    rows = t_lat // GRID_W
    row = jnp.repeat(jnp.arange(rows, dtype=jnp.int32), GRID_W)
    col = jnp.tile(jnp.arange(GRID_W, dtype=jnp.int32), rows)
    nfreq = HEAD_DIM // 4
    inv_freq = ROPE_THETA ** (-jnp.arange(nfreq, dtype=F32) / nfreq)
    ang_r = row.astype(F32)[:, None] * inv_freq
    ang_c = col.astype(F32)[:, None] * inv_freq
    cos = jnp.concatenate([jnp.cos(ang_r)] * 2 + [jnp.cos(ang_c)] * 2, axis=1)
    sin = jnp.concatenate([-jnp.sin(ang_r), jnp.sin(ang_r), -jnp.sin(ang_c), jnp.sin(ang_c)], axis=1)
    reps = LANES // HEAD_DIM
    cos = jnp.concatenate([jnp.ones((t_ctx, LANES), F32), jnp.tile(cos, (1, reps))], axis=0)
    sin = jnp.concatenate([jnp.zeros((t_ctx, LANES), F32), jnp.tile(sin, (1, reps))], axis=0)
    return jnp.tile(cos, (batch, 1)), jnp.tile(sin, (batch, 1))


def kernel(x, c, ctx, c_ctx, ada_w, ada_b, norm_gain, w_in, conv_w, rw_w0, rw_w_up, rw_a0, rw_a_up,
           rw_g_up, rw_k_k, rw_k_a, rw_r_k, rw_gn_gain, rw_gn_bias, da_q_gain, da_k_gain, da_lambda,
           da_sub_gain, w_out, peer_wq, peer_keys, peer_u, peer_v):
    batch, t_lat, d = x.shape
    t_ctx = ctx.shape[1]
    seq = t_ctx + t_lat
    n = batch * seq
    depth = ada_w.shape[0]
    rw = rw_k_k.shape[1]
    lora_w = rw_w_up.shape[2] + rw_a_up.shape[2] + rw_g_up.shape[1]
    da_w = w_out.shape[1] - rw
    p_heads, _, nk, half = peer_keys.shape[1:]
    assert 2 * batch * (rw // HEAD_DIM) == LANES, "the recurrence kernel fills the lanes with (half, batch, head)"

    tm = _tile(math.gcd(t_lat, t_ctx), 256)
    tile_pos = (jnp.arange(n // tm, dtype=jnp.int32) * tm) % seq
    gid = jnp.where(tile_pos < t_ctx, batch, jnp.arange(n // tm, dtype=jnp.int32) * tm // seq)
    first = ((tile_pos == 0) | (tile_pos == t_ctx)).astype(jnp.int32)
    last = ((tile_pos + tm == t_ctx) | (tile_pos + tm == seq)).astype(jnp.int32)

    xs = jnp.concatenate([ctx, x], axis=1).reshape(n, d)
    c8 = jnp.concatenate([c, c_ctx[None], jnp.zeros((8 - batch - 1, d), F32)], axis=0)
    mods = _modulation(c8, ada_w, ada_b)[:, :batch + 1].reshape(depth, batch + 1, 6, 1, d)
    cos_t, sin_t = _rope_tables(t_lat, t_ctx, batch)

    cols = jnp.concatenate([jnp.arange(3 * rw), jnp.arange(3 * rw + lora_w, w_in.shape[2]),
                            jnp.arange(3 * rw, 3 * rw + lora_w)])
    tn_in = _tile(w_in.shape[2], 1280, LANES)
    ts = _tile(tm, 128)

    for l in range(depth):
        lam_init = 0.8 - 0.6 * math.exp(-0.3 * l)
        mod = [mods[l, :, j] for j in range(6)]

        p = _norm_proj_wide(xs, gid, norm_gain[l, 0], mod[0], mod[1], w_in[l][:, cols].astype(BF16), tm,
                            _tile(n, 1024), tn_in)
        q8, v, bonus, gate = _rwkv_prep(
            p, first, last, conv_w[l], rw_w0[l], rw_w_up[l], rw_a0[l], rw_a_up[l], rw_g_up[l],
            rw_k_k[l], rw_k_a[l], rw_r_k[l].reshape(-1), tm)
        qs, vs = _to_streams(q8, v, batch, ts)
        yf, yb = _rwkv_scan(qs, vs, t_ctx, tc=_tile(math.gcd(t_lat, t_ctx), 64))
        rw_out = _rwkv_finish(yf, yb, bonus, gate, rw_gn_gain[l], rw_gn_bias[l], batch, ts)

        qkv = _da_prep(p, cos_t, sin_t, da_q_gain[l], da_k_gain[l], 3 * rw, da_w, tm)
        da_out = _diff_attention(qkv, da_lambda[l], da_sub_gain[l], lam_init, batch, t_ctx, tm)
        xs = _gated_proj(rw_out, da_out, w_out[l].astype(BF16), xs, gid, mod[2], tm)

        h, scores = _peer_query(xs, gid, norm_gain[l, 1], mod[3], mod[4], peer_wq[l].astype(BF16),
                                peer_keys[l].reshape(2 * p_heads, nk, half), tm)
        e1, bi, e2, r2 = _peer_select(scores, p_heads, _tile(n, 1024, LANES))
        out = _peer_dense(h, peer_u[l].astype(BF16), peer_v[l].astype(BF16), e1, bi, e2, r2,
                          tl=_tile(n, 512, LANES), et=4 * nk, chunk=2 * nk)
        xs = _gated_add(xs, out, gid, mod[5], tm)

    return xs.reshape(batch, seq, d)[:, t_ctx:]
```
